```python
import math
import jax
import jax.numpy as jnp
from jax import lax
import numpy as np

D_MODEL = 1024
BATCH = 2
SEQ = 8192
DEPTH = 1

CHUNK = 64
EPS = 1e-6
NEG_INF = -1e30
ATT_HEADS = 8
ATT_HEAD_DIM = 64
ATT_WIDTH = ATT_HEADS * ATT_HEAD_DIM
LEFT_CHUNKS = 8
BAND_CHUNKS = LEFT_CHUNKS + 1
BAND = BAND_CHUNKS * CHUNK
REL_MAX = 128
REL_BUCKETS = (CHUNK - 1) + REL_MAX + 1
SSM_WIDTH = D_MODEL - ATT_WIDTH
SSM_GROUP = 16
SSM_GROUPS = SSM_WIDTH // SSM_GROUP
SSM_STATE = 64
MIX_WIDTH = ATT_WIDTH + SSM_WIDTH
IN_WIDTH = 3 * ATT_WIDTH + SSM_WIDTH
MEM_LEN = 256
MEM_HEADS = 4
MEM_HEAD_DIM = D_MODEL // MEM_HEADS
PEER_HEADS = 8
PEER_KEYS = 128
PEER_EXPERTS = PEER_KEYS * PEER_KEYS
PEER_TOPK = 16
PEER_QDIM = 256
PEER_BLOCK = 128

kernel_name = 'hybrid_chunkattn_s5_peer_block'


def _rmsnorm(x, g):
    xf = x.astype(jnp.float32)
    y = xf * lax.rsqrt(jnp.mean(xf * xf, axis=-1, keepdims=True) + EPS)
    return (y * g.astype(jnp.float32)).astype(x.dtype)


def _band_bias(rel_bias):
    qi = jnp.arange(CHUNK)
    kj = jnp.arange(BAND)
    dist = LEFT_CHUNKS * CHUNK + qi[:, None] - kj[None, :]
    bucket = jnp.clip(dist, -(CHUNK - 1), REL_MAX) + (CHUNK - 1)
    return rel_bias[:, bucket]


def _chunked_attention(q, k, v, rel_bias):
    bsz, seq = q.shape[0], q.shape[1]
    nc = seq // CHUNK
    shp = (bsz, nc, CHUNK, ATT_HEADS, ATT_HEAD_DIM)
    qc = q.reshape(shp)
    pad = ((0, 0), (LEFT_CHUNKS, 0), (0, 0), (0, 0), (0, 0))
    kp = jnp.pad(k.reshape(shp), pad)
    vp = jnp.pad(v.reshape(shp), pad)
    idx = jnp.arange(nc)[:, None] + jnp.arange(BAND_CHUNKS)[None, :]
    band_shp = (bsz, nc, BAND, ATT_HEADS, ATT_HEAD_DIM)
    kb = kp[:, idx].reshape(band_shp)
    vb = vp[:, idx].reshape(band_shp)
    s = jnp.einsum('bnqhd,bnkhd->bhnqk', qc, kb).astype(jnp.float32) * (ATT_HEAD_DIM ** -0.5)
    s = s + _band_bias(rel_bias).astype(jnp.float32)[None, :, None]
    key_chunk = jnp.arange(nc)[:, None] - LEFT_CHUNKS + (jnp.arange(BAND) // CHUNK)[None, :]
    s = jnp.where((key_chunk >= 0)[None, None, :, None, :], s, NEG_INF)
    p = jax.nn.softmax(s, axis=-1).astype(v.dtype)
    o = jnp.einsum('bhnqk,bnkhd->bnqhd', p, vb)
    return o.reshape(bsz, seq, ATT_WIDTH)


def _complex_affine_combine(c1, c2):
    a1r, a1i, b1r, b1i = c1
    a2r, a2i, b2r, b2i = c2
    ar = a2r * a1r - a2i * a1i
    ai = a2r * a1i + a2i * a1r
    br = a2r * b1r - a2i * b1i + b2r
    bi = a2r * b1i + a2i * b1r + b2i
    return (ar, ai, br, bi)


def _s5_mixer(u, lam_re, lam_im, log_step, b_re, b_im, c_re, c_im, d_skip, w_glu, b_glu):
    bsz, seq = u.shape[0], u.shape[1]
    uf = u.astype(jnp.float32).reshape(bsz, seq, SSM_GROUPS, SSM_GROUP)
    step = jnp.exp(log_step.astype(jnp.float32))[:, None]
    lr = lam_re.astype(jnp.float32)
    li = lam_im.astype(jnp.float32)
    mag = jnp.exp(lr * step)
    ar = mag * jnp.cos(li * step)
    ai = mag * jnp.sin(li * step)
    zr = ar - 1.0
    den = lr * lr + li * li
    fr = (zr * lr + ai * li) / den
    fi = (ai * lr - zr * li) / den
    br = b_re.astype(jnp.float32)
    bi = b_im.astype(jnp.float32)
    bbr = fr[..., None] * br - fi[..., None] * bi
    bbi = fr[..., None] * bi + fi[..., None] * br
    bu_r = jnp.einsum('blgc,gpc->blgp', uf, bbr)
    bu_i = jnp.einsum('blgc,gpc->blgp', uf, bbi)
    a_r = jnp.broadcast_to(ar, bu_r.shape)
    a_i = jnp.broadcast_to(ai, bu_i.shape)
    _, _, xr, xi = lax.associative_scan(_complex_affine_combine, (a_r, a_i, bu_r, bu_i), axis=1)
    y = (jnp.einsum('blgp,gcp->blgc', xr, c_re.astype(jnp.float32))
         - jnp.einsum('blgp,gcp->blgc', xi, c_im.astype(jnp.float32))
         + d_skip.astype(jnp.float32) * uf)
    y = jax.nn.gelu(y.reshape(bsz, seq, SSM_WIDTH), approximate=False)
    y = y * jax.nn.sigmoid(y @ w_glu.astype(jnp.float32) + b_glu.astype(jnp.float32))
    return y.astype(u.dtype)


def _memory_attention(hn, memn, w_q, w_k, w_v, q_g, k_g, w_o):
    bsz, seq = hn.shape[0], hn.shape[1]
    m = memn.shape[1]
    q = _rmsnorm((hn @ w_q).reshape(bsz, seq, MEM_HEADS, MEM_HEAD_DIM), q_g)
    k = _rmsnorm((memn @ w_k).reshape(bsz, m, MEM_HEADS, MEM_HEAD_DIM), k_g)
    v = (memn @ w_v).reshape(bsz, m, MEM_HEADS, MEM_HEAD_DIM)
    s = jnp.einsum('blhd,bmhd->bhlm', q, k).astype(jnp.float32) * (MEM_HEAD_DIM ** -0.5)
    p = jax.nn.softmax(s, axis=-1).astype(v.dtype)
    o = jnp.einsum('bhlm,bmhd->blhd', p, v).reshape(bsz, seq, D_MODEL)
    return o @ w_o


def _peer_ffn(hn, w_q, sub_keys, u_tab, v_tab):
    bsz, seq, d = hn.shape
    tokens = hn.reshape(-1, PEER_BLOCK, d)

    def block(xb):
        q = (xb @ w_q).reshape(PEER_BLOCK, PEER_HEADS, 2, PEER_QDIM // 2)
        s = jnp.einsum('thsd,hsnd->thsn', q, sub_keys).astype(jnp.float32)
        sv, si = lax.top_k(s, PEER_TOPK)
        cand = (sv[:, :, 0, :, None] + sv[:, :, 1, None, :]).reshape(PEER_BLOCK, PEER_HEADS, -1)
        cidx = (si[:, :, 0, :, None] * PEER_KEYS + si[:, :, 1, None, :]).reshape(PEER_BLOCK, PEER_HEADS, -1)
        top, pos = lax.top_k(cand, PEER_TOPK)
        e = jnp.take_along_axis(cidx, pos, axis=-1)
        g = jax.nn.softmax(top, axis=-1)
        act = jax.nn.gelu(jnp.einsum('thkd,td->thk', u_tab[e], xb), approximate=False)
        w = (g * act).astype(xb.dtype)
        return jnp.einsum('thk,thkd->td', w, v_tab[e])

    return lax.map(block, tokens).reshape(bsz, seq, d)


def setup_inputs(seed: int = 0) -> dict:
    key = jax.random.key(seed)
    ks = iter(jax.random.split(key, 40))

    def nrm(shape, scale):
        return scale * jax.random.normal(next(ks), shape, jnp.float32)

    def gain(n):
        return 1.0 + nrm((DEPTH, n), 0.02)

    x = nrm((BATCH, SEQ, D_MODEL), 1.0)
    mem = nrm((BATCH, MEM_LEN, D_MODEL), 1.0)
    norm_mix_g = gain(D_MODEL)
    w_in = nrm((DEPTH, D_MODEL, IN_WIDTH), D_MODEL ** -0.5)
    att_q_g = gain(ATT_HEAD_DIM)
    att_k_g = gain(ATT_HEAD_DIM)
    rel_bias = nrm((DEPTH, ATT_HEADS, REL_BUCKETS), 0.5)
    ssm_lam_re = -0.5 + nrm((DEPTH, SSM_GROUPS, SSM_STATE), 0.01)
    ssm_lam_im = math.pi * jnp.arange(SSM_STATE, dtype=jnp.float32) + nrm((DEPTH, SSM_GROUPS, SSM_STATE), 0.01)
    ssm_log_step = jax.random.uniform(next(ks), (DEPTH, SSM_GROUPS), jnp.float32, math.log(1e-3), math.log(1e-1))
    ssm_b_re = nrm((DEPTH, SSM_GROUPS, SSM_STATE, SSM_GROUP), (2 * SSM_GROUP) ** -0.5)
    ssm_b_im = nrm((DEPTH, SSM_GROUPS, SSM_STATE, SSM_GROUP), (2 * SSM_GROUP) ** -0.5)
    ssm_c_re = nrm((DEPTH, SSM_GROUPS, SSM_GROUP, SSM_STATE), (2 * SSM_STATE) ** -0.5)
    ssm_c_im = nrm((DEPTH, SSM_GROUPS, SSM_GROUP, SSM_STATE), (2 * SSM_STATE) ** -0.5)
    ssm_d = nrm((DEPTH, SSM_GROUPS, SSM_GROUP), 1.0)
    ssm_w_glu = nrm((DEPTH, SSM_WIDTH, SSM_WIDTH), SSM_WIDTH ** -0.5)
    ssm_b_glu = nrm((DEPTH, SSM_WIDTH), 0.01)
    att_out_g = gain(ATT_WIDTH)
    ssm_out_g = gain(SSM_WIDTH)
    w_out = nrm((DEPTH, MIX_WIDTH, D_MODEL), MIX_WIDTH ** -0.5)
    norm_mem_g = gain(D_MODEL)
    norm_memkv_g = gain(D_MODEL)
    w_mem_q = nrm((DEPTH, D_MODEL, D_MODEL), D_MODEL ** -0.5)
    w_mem_k = nrm((DEPTH, D_MODEL, D_MODEL), D_MODEL ** -0.5)
    w_mem_v = nrm((DEPTH, D_MODEL, D_MODEL), D_MODEL ** -0.5)
    mem_q_g = gain(MEM_HEAD_DIM)
    mem_k_g = gain(MEM_HEAD_DIM)
    w_mem_o = nrm((DEPTH, D_MODEL, D_MODEL), D_MODEL ** -0.5)
    norm_peer_g = gain(D_MODEL)
    w_peer_q = nrm((DEPTH, D_MODEL, PEER_HEADS * PEER_QDIM), D_MODEL ** -0.5)
    peer_keys = nrm((DEPTH, PEER_HEADS, 2, PEER_KEYS, PEER_QDIM // 2), (PEER_QDIM // 2) ** -0.5)
    peer_u = nrm((DEPTH, PEER_EXPERTS, D_MODEL), D_MODEL ** -0.5)
    peer_v = nrm((DEPTH, PEER_EXPERTS, D_MODEL), PEER_HEADS ** -0.5)
    return {'x': x, 'mem': mem, 'norm_mix_g': norm_mix_g, 'w_in': w_in,
            'att_q_g': att_q_g, 'att_k_g': att_k_g, 'rel_bias': rel_bias,
            'ssm_lam_re': ssm_lam_re, 'ssm_lam_im': ssm_lam_im, 'ssm_log_step': ssm_log_step,
            'ssm_b_re': ssm_b_re, 'ssm_b_im': ssm_b_im, 'ssm_c_re': ssm_c_re, 'ssm_c_im': ssm_c_im,
            'ssm_d': ssm_d, 'ssm_w_glu': ssm_w_glu, 'ssm_b_glu': ssm_b_glu,
            'att_out_g': att_out_g, 'ssm_out_g': ssm_out_g, 'w_out': w_out,
            'norm_mem_g': norm_mem_g, 'norm_memkv_g': norm_memkv_g,
            'w_mem_q': w_mem_q, 'w_mem_k': w_mem_k, 'w_mem_v': w_mem_v,
            'mem_q_g': mem_q_g, 'mem_k_g': mem_k_g, 'w_mem_o': w_mem_o,
            'norm_peer_g': norm_peer_g, 'w_peer_q': w_peer_q, 'peer_keys': peer_keys,
            'peer_u': peer_u, 'peer_v': peer_v}


def reference(x, mem, norm_mix_g, w_in, att_q_g, att_k_g, rel_bias,
              ssm_lam_re, ssm_lam_im, ssm_log_step, ssm_b_re, ssm_b_im, ssm_c_re, ssm_c_im,
              ssm_d, ssm_w_glu, ssm_b_glu, att_out_g, ssm_out_g, w_out,
              norm_mem_g, norm_memkv_g, w_mem_q, w_mem_k, w_mem_v, mem_q_g, mem_k_g, w_mem_o,
              norm_peer_g, w_peer_q, peer_keys, peer_u, peer_v):
    bsz, seq = x.shape[0], x.shape[1]
    h = x
    for l in range(DEPTH):
        proj = _rmsnorm(h, norm_mix_g[l]) @ w_in[l]
        q = proj[..., :ATT_WIDTH].reshape(bsz, seq, ATT_HEADS, ATT_HEAD_DIM)
        k = proj[..., ATT_WIDTH:2 * ATT_WIDTH].reshape(bsz, seq, ATT_HEADS, ATT_HEAD_DIM)
        v = proj[..., 2 * ATT_WIDTH:3 * ATT_WIDTH].reshape(bsz, seq, ATT_HEADS, ATT_HEAD_DIM)
        u = proj[..., 3 * ATT_WIDTH:]
        att = _chunked_attention(_rmsnorm(q, att_q_g[l]), _rmsnorm(k, att_k_g[l]), v, rel_bias[l])
        ssm = _s5_mixer(u, ssm_lam_re[l], ssm_lam_im[l], ssm_log_step[l], ssm_b_re[l], ssm_b_im[l],
                        ssm_c_re[l], ssm_c_im[l], ssm_d[l], ssm_w_glu[l], ssm_b_glu[l])
        mixed = jnp.concatenate([_rmsnorm(att, att_out_g[l]), _rmsnorm(ssm, ssm_out_g[l])], axis=-1)
        h = h + mixed @ w_out[l]
        h = h + _memory_attention(_rmsnorm(h, norm_mem_g[l]), _rmsnorm(mem, norm_memkv_g[l]),
                                  w_mem_q[l], w_mem_k[l], w_mem_v[l], mem_q_g[l], mem_k_g[l], w_mem_o[l])
        h = h + _peer_ffn(_rmsnorm(h, norm_peer_g[l]), w_peer_q[l], peer_keys[l], peer_u[l], peer_v[l])
    return h
```

```python
import functools
import math

import numpy as np
import jax
import jax.numpy as jnp
from jax import lax
from jax.experimental import pallas as pl
from jax.experimental.pallas import tpu as pltpu

F32 = jnp.float32
BF16 = jnp.bfloat16

EPS = 1e-6
NEG_INF = -1e30

CHUNK = 64
ATT_HEADS = 8
ATT_HEAD_DIM = 64
ATT_WIDTH = ATT_HEADS * ATT_HEAD_DIM
LEFT_CHUNKS = 8
REL_MAX = 128
SSM_GROUP = 16
SSM_GROUPS = 32
SSM_STATE = 64
SSM_WIDTH = SSM_GROUP * SSM_GROUPS
MEM_HEADS = 4
PEER_HEADS = 8
PEER_KEYS = 128
PEER_TOPK = 16

INPROJ_ROWS = 512
ATT_QBLOCK = 512
ATT_GROUP = 256
ATT_WINDOW = ATT_GROUP + LEFT_CHUNKS * CHUNK
SSM_T = 32
MIX_ROWS = 512
MEM_ROWS = 512
PEER_SEL_TOKENS = 256
PEER_TOKENS = 512
PEER_EXPERT_TILE = 1024

VMEM_LIMIT = 48 * 1024 * 1024


def _cparams(sem):
    return pltpu.CompilerParams(dimension_semantics=sem, vmem_limit_bytes=VMEM_LIMIT)


def _rms_rows(x, gain):
    ms = jnp.mean(x * x, axis=-1, keepdims=True)
    return x * lax.rsqrt(ms + EPS) * gain


def _dot(a, b):
    return jnp.dot(a, b, preferred_element_type=F32)


def _dot_nt(a, b):
    return lax.dot_general(a, b, (((1,), (1,)), ((), ())), preferred_element_type=F32)


def _inproj_kernel(x_ref, g_ref, w_ref, seg_ref, qg_ref, kg_ref, q_ref, k_ref, v_ref, u_ref):
    xn = _rms_rows(x_ref[...], g_ref[...])
    proj = _dot(xn.astype(BF16), w_ref[...])
    seg = seg_ref[...]

    def head_norm(z, gain):
        zz = z * z
        hi = zz.astype(BF16)
        lo = (zz - hi.astype(F32)).astype(BF16)
        ss = _dot(hi, seg) + _dot(lo, seg)
        return z * lax.rsqrt(ss * (1.0 / ATT_HEAD_DIM) + EPS) * gain

    q = head_norm(proj[:, :ATT_WIDTH], qg_ref[...]) * (ATT_HEAD_DIM ** -0.5)
    q_ref[...] = q.astype(BF16)
    k_ref[...] = head_norm(proj[:, ATT_WIDTH:2 * ATT_WIDTH], kg_ref[...]).astype(BF16)
    v_ref[...] = proj[:, 2 * ATT_WIDTH:3 * ATT_WIDTH].astype(BF16)
    u_ref[...] = proj[:, 3 * ATT_WIDTH:]


def _inproj(x2, g, w_in_bf, seg, qg, kg):
    n, d = x2.shape
    tm = INPROJ_ROWS
    row = lambda i: (i, 0)
    fix = lambda i: (0, 0)
    outw = ATT_WIDTH
    return pl.pallas_call(
        _inproj_kernel,
        grid=(n // tm,),
        in_specs=[pl.BlockSpec((tm, d), row), pl.BlockSpec((1, d), fix),
                  pl.BlockSpec(w_in_bf.shape, fix), pl.BlockSpec(seg.shape, fix),
                  pl.BlockSpec((1, outw), fix), pl.BlockSpec((1, outw), fix)],
        out_specs=[pl.BlockSpec((tm, outw), row)] * 3 + [pl.BlockSpec((tm, SSM_WIDTH), row)],
        out_shape=[jax.ShapeDtypeStruct((n, outw), BF16)] * 3
        + [jax.ShapeDtypeStruct((n, SSM_WIDTH), F32)],
        compiler_params=_cparams(("parallel",)),
        name="inproj",
    )(x2, g, w_in_bf, seg, qg, kg)


def _attn_kernel(q_ref, kp_ref, kc_ref, vp_ref, vc_ref, bias_ref, o_ref):
    blk = pl.program_id(1)
    kk = jnp.concatenate([kp_ref[...], kc_ref[...]], axis=0)
    vv = jnp.concatenate([vp_ref[...], vc_ref[...]], axis=0)
    chunks_per_block = ATT_QBLOCK // CHUNK
    key_chunk = lax.broadcasted_iota(jnp.int32, (1, ATT_WINDOW), 1) // CHUNK
    for gi in range(ATT_QBLOCK // ATT_GROUP):
        r0 = gi * ATT_GROUP
        q = q_ref[r0:r0 + ATT_GROUP, :]
        kwin = kk[r0:r0 + ATT_WINDOW, :]
        vwin = vv[r0:r0 + ATT_WINDOW, :]
        abs_chunk = blk * chunks_per_block + gi * (ATT_GROUP // CHUNK) + key_chunk - LEFT_CHUNKS
        valid = abs_chunk >= 0
        for h in range(ATT_HEADS):
            c0 = h * ATT_HEAD_DIM
            s = _dot_nt(q[:, c0:c0 + ATT_HEAD_DIM], kwin[:, c0:c0 + ATT_HEAD_DIM])
            s = jnp.where(valid, s + bias_ref[h], NEG_INF)
            m = jnp.max(s, axis=-1, keepdims=True)
            p = jnp.exp(s - m)
            l = jnp.sum(p, axis=-1, keepdims=True)
            o = _dot(p.astype(BF16), vwin[:, c0:c0 + ATT_HEAD_DIM])
            o_ref[r0:r0 + ATT_GROUP, c0:c0 + ATT_HEAD_DIM] = o / l


def _attention(q, k, v, bias, bsz, seq):
    qb = ATT_QBLOCK
    cur = lambda b, i: (b, i, 0)
    prev = lambda b, i: (b, jnp.maximum(i - 1, 0), 0)
    blk = (None, qb, ATT_WIDTH)
    return pl.pallas_call(
        _attn_kernel,
        grid=(bsz, seq // qb),
        in_specs=[pl.BlockSpec(blk, cur), pl.BlockSpec(blk, prev), pl.BlockSpec(blk, cur),
                  pl.BlockSpec(blk, prev), pl.BlockSpec(blk, cur),
                  pl.BlockSpec(bias.shape, lambda b, i: (0, 0, 0))],
        out_specs=pl.BlockSpec(blk, cur),
        out_shape=jax.ShapeDtypeStruct((bsz, seq, ATT_WIDTH), F32),
        compiler_params=_cparams(("parallel", "parallel")),
        name="attn",
    )(q, k, k, v, v, bias)


def _band_bias_tile(rel_bias):
    r = np.arange(ATT_GROUP)[:, None]
    s = np.arange(ATT_WINDOW)[None, :]
    dist = LEFT_CHUNKS * CHUNK + r - s
    bucket = np.clip(dist, -(CHUNK - 1), REL_MAX) + (CHUNK - 1)
    qc, kc = r // CHUNK, s // CHUNK
    band = (kc >= qc) & (kc <= qc + LEFT_CHUNKS)
    return jnp.where(band[None], rel_bias[:, bucket], NEG_INF)


def _ssm_param_kernel(lr_r_ref, li_r_ref, lr_c_ref, li_c_ref, ls_ref, taus_ref,
                      bre_ref, bie_ref, cre_ref, cie_ref, brt_ref, bit_ref, crt_ref, cit_ref,
                      k_ref, bcat_ref, ccat_ref, apr_ref, api_ref):
    t = SSM_T
    p = SSM_STATE
    step = jnp.exp(ls_ref[...])
    lr_r, li_r = lr_r_ref[...], li_r_ref[...]
    lr_c, li_c = lr_c_ref[...], li_c_ref[...]

    def zoh(lr, li):
        mag = jnp.exp(lr * step)
        ar = mag * jnp.cos(li * step)
        ai = mag * jnp.sin(li * step)
        zr = ar - 1.0
        den = lr * lr + li * li
        return (zr * lr + ai * li) / den, (ai * lr - zr * li) / den

    def power(lr, li, tau):
        mag = jnp.exp(lr * step * tau)
        ph = li * step * tau
        return mag * jnp.cos(ph), mag * jnp.sin(ph)

    fr_r, fi_r = zoh(lr_r, li_r)
    fr_c, fi_c = zoh(lr_c, li_c)

    bre, bie = bre_ref[...], bie_ref[...]
    cre, cie = cre_ref[...], cie_ref[...]
    bbr = fr_c * bre - fi_c * bie
    bbi = fr_c * bie + fi_c * bre
    cb_re = cre * bbr - cie * bbi
    cb_im = cre * bbi + cie * bbr
    tau = lax.broadcasted_iota(jnp.int32, (t, 1), 0).astype(F32)
    pw_r, pw_i = power(lr_r, li_r, tau)
    hp = lax.Precision.HIGHEST
    k_ref[...] = (jnp.dot(pw_r, cb_re, precision=hp, preferred_element_type=F32)
                  - jnp.dot(pw_i, cb_im, precision=hp, preferred_element_type=F32))

    brt, bit = brt_ref[...], bit_ref[...]
    bbr_t = fr_r * brt - fi_r * bit
    bbi_t = fr_r * bit + fi_r * brt
    rv_r, rv_i = power(lr_r, li_r, float(t - 1) - tau)
    b_re = rv_r[:, None, :] * bbr_t[None] - rv_i[:, None, :] * bbi_t[None]
    b_im = rv_r[:, None, :] * bbi_t[None] + rv_i[:, None, :] * bbr_t[None]
    bcat_ref[:, 0:p] = b_re.reshape(t * SSM_GROUP, p)
    bcat_ref[:, p:2 * p] = b_im.reshape(t * SSM_GROUP, p)

    lane_t = (lax.broadcasted_iota(jnp.int32, (1, t * SSM_GROUP), 1) // SSM_GROUP + 1).astype(F32)
    p1_r, p1_i = power(lr_c, li_c, lane_t)
    crt, cit = crt_ref[...], cit_ref[...]
    ccat_ref[0:p, :] = crt * p1_r - cit * p1_i
    ccat_ref[p:2 * p, :] = -(crt * p1_i + cit * p1_r)

    dr, di = power(lr_r, li_r, taus_ref[...])
    apr_ref[:, 0:p] = dr
    apr_ref[:, p:2 * p] = dr
    api_ref[:, 0:p] = -di
    api_ref[:, p:2 * p] = di


def _ssm_params(lam_re, lam_im, log_step, b_re, b_im, c_re, c_im, n_steps):
    g, p, c = SSM_GROUPS, SSM_STATE, SSM_GROUP
    t = SSM_T
    taus = jnp.asarray([[float(t * 2 ** k)] for k in range(n_steps)], F32)
    c_re_t = jnp.swapaxes(c_re, 1, 2)
    c_im_t = jnp.swapaxes(c_im, 1, 2)
    args = [
        lam_re[:, None, :], lam_im[:, None, :], lam_re[:, :, None], lam_im[:, :, None],
        log_step[:, None, None], taus,
        jnp.tile(b_re, (1, 1, c)), jnp.tile(b_im, (1, 1, c)),
        jnp.repeat(c_re_t, c, axis=2), jnp.repeat(c_im_t, c, axis=2),
        jnp.swapaxes(b_re, 1, 2), jnp.swapaxes(b_im, 1, 2),
        jnp.tile(c_re_t, (1, 1, t)), jnp.tile(c_im_t, (1, 1, t)),
    ]

    def spec(a):
        if a.ndim == 2:
            return pl.BlockSpec(a.shape, lambda i: (0, 0))
        return pl.BlockSpec((None,) + a.shape[1:], lambda i: (i, 0, 0))

    out_dims = [(t, c * c), (t * c, 2 * p), (2 * p, t * c), (n_steps, 2 * p), (n_steps, 2 * p)]
    return pl.pallas_call(
        _ssm_param_kernel,
        grid=(g,),
        in_specs=[spec(a) for a in args],
        out_specs=[pl.BlockSpec((None,) + d, lambda i: (i, 0, 0)) for d in out_dims],
        out_shape=[jax.ShapeDtypeStruct((g,) + d, F32) for d in out_dims],
        compiler_params=_cparams(("parallel",)),
        name="ssm_params",
    )(*args)


def _toeplitz(kflat):
    g, t, c = SSM_GROUPS, SSM_T, SSM_GROUP
    kf = kflat.reshape(g, t, c, c)
    ss = np.arange(t)[:, None]
    tt = np.arange(t)[None, :]
    lag = np.clip(tt - ss, 0, t - 1)
    m = kf[:, lag]
    m = jnp.where((tt >= ss)[None, :, :, None, None], m, 0.0)
    return m.transpose(0, 1, 4, 2, 3).reshape(g, t * c, t * c)


def _ssm_main_kernel(u_ref, m_ref, bcat_ref, ccat_ref, apr_ref, api_ref, d_ref, y_ref,
                     *, rows_per_batch, n_steps):
    u = u_ref[...]
    ub = u.astype(BF16)
    y = _dot(ub, m_ref[...])
    x = _dot(ub, bcat_ref[...].astype(BF16))
    r = x.shape[0]
    k_in_batch = lax.broadcasted_iota(jnp.int32, (r, 1), 0) % rows_per_batch
    half = SSM_STATE
    for k in range(n_steps):
        d = 2 ** k
        sh = jnp.where(k_in_batch >= d, pltpu.roll(x, d, 0), 0.0)
        x = x + sh * apr_ref[k:k + 1, :] + pltpu.roll(sh, half, 1) * api_ref[k:k + 1, :]
    x_prev = jnp.where(k_in_batch >= 1, pltpu.roll(x, 1, 0), 0.0)
    y = y + _dot(x_prev.astype(BF16), ccat_ref[...].astype(BF16))
    y_ref[...] = y + u * d_ref[...]


def _ssm_main(u_g, m_bf, bcat, ccat, apr, api, d_flat, rows_per_batch, n_steps):
    g, r, w = u_g.shape
    per_g = lambda a: pl.BlockSpec((None,) + a.shape[1:], lambda i: (i, 0, 0))
    kern = functools.partial(_ssm_main_kernel, rows_per_batch=rows_per_batch, n_steps=n_steps)
    return pl.pallas_call(
        kern,
        grid=(g,),
        in_specs=[per_g(a) for a in (u_g, m_bf, bcat, ccat, apr, api, d_flat)],
        out_specs=per_g(u_g),
        out_shape=jax.ShapeDtypeStruct((g, r, w), F32),
        compiler_params=_cparams(("parallel",)),
        name="ssm_main",
    )(u_g, m_bf, bcat, ccat, apr, api, d_flat)


def _gelu(x):
    return 0.5 * x * (1.0 + lax.erf(x * (1.0 / math.sqrt(2.0))))


def _mix_out_kernel(x_ref, att_ref, y_ref, wg_ref, bg_ref, ag_ref, sg_ref, wa_ref, ws_ref, h_ref):
    y = _gelu(y_ref[...])
    z = _dot(y.astype(BF16), wg_ref[...]) + bg_ref[...]
    ssm = y * jax.nn.sigmoid(z)
    a = _rms_rows(att_ref[...], ag_ref[...])
    s = _rms_rows(ssm, sg_ref[...])
    h_ref[...] = x_ref[...] + _dot(a.astype(BF16), wa_ref[...]) + _dot(s.astype(BF16), ws_ref[...])


def _mix_out(x2, att, y, w_glu_bf, b_glu, att_g, ssm_g, w_out_a, w_out_s):
    n, d = x2.shape
    tm = MIX_ROWS
    row = lambda i: (i, 0)
    fix = lambda i: (0, 0)
    full = lambda a: pl.BlockSpec(a.shape, fix)
    return pl.pallas_call(
        _mix_out_kernel,
        grid=(n // tm,),
        in_specs=[pl.BlockSpec((tm, d), row), pl.BlockSpec((tm, ATT_WIDTH), row),
                  pl.BlockSpec((tm, SSM_WIDTH), row), full(w_glu_bf), full(b_glu), full(att_g),
                  full(ssm_g), full(w_out_a), full(w_out_s)],
        out_specs=pl.BlockSpec((tm, d), row),
        out_shape=jax.ShapeDtypeStruct((n, d), F32),
        compiler_params=_cparams(("parallel",)),
        name="mix_out",
    )(x2, att, y, w_glu_bf, b_glu, att_g, ssm_g, w_out_a, w_out_s)


def _mem_kv_kernel(mem_ref, g_ref, wk_ref, wv_ref, kg_ref, k_ref, v_ref):
    mn = _rms_rows(mem_ref[...], g_ref[...]).astype(BF16)
    k = _dot(mn, wk_ref[...])
    hd = k.shape[1] // MEM_HEADS
    for h in range(MEM_HEADS):
        k_ref[:, h * hd:(h + 1) * hd] = _rms_rows(k[:, h * hd:(h + 1) * hd], kg_ref[...]).astype(BF16)
    v_ref[...] = _dot(mn, wv_ref[...]).astype(BF16)


def _mem_kv(mem, g, wk_bf, wv_bf, kg):
    bsz, m, d = mem.shape
    fix = lambda b: (0, 0)
    per_b = pl.BlockSpec((None, m, d), lambda b: (b, 0, 0))
    return pl.pallas_call(
        _mem_kv_kernel,
        grid=(bsz,),
        in_specs=[per_b, pl.BlockSpec(g.shape, fix), pl.BlockSpec(wk_bf.shape, fix),
                  pl.BlockSpec(wv_bf.shape, fix), pl.BlockSpec(kg.shape, fix)],
        out_specs=[per_b, per_b],
        out_shape=[jax.ShapeDtypeStruct((bsz, m, d), BF16)] * 2,
        compiler_params=_cparams(("parallel",)),
        name="mem_kv",
    )(mem, g, wk_bf, wv_bf, kg)


def _mem_attn_kernel(h_ref, g_ref, wq_ref, qg_ref, k_ref, v_ref, wo_ref, o_ref):
    h1 = h_ref[...]
    hn = _rms_rows(h1, g_ref[...]).astype(BF16)
    q = _dot(hn, wq_ref[...])
    hd = q.shape[1] // MEM_HEADS
    outs = []
    for h in range(MEM_HEADS):
        sl = slice(h * hd, (h + 1) * hd)
        qh = (_rms_rows(q[:, sl], qg_ref[...]) * (hd ** -0.5)).astype(BF16)
        s = _dot_nt(qh, k_ref[:, sl])
        m = jnp.max(s, axis=-1, keepdims=True)
        p = jnp.exp(s - m)
        l = jnp.sum(p, axis=-1, keepdims=True)
        outs.append(_dot(p.astype(BF16), v_ref[:, sl]) / l)
    o = jnp.concatenate(outs, axis=-1).astype(BF16)
    o_ref[...] = h1 + _dot(o, wo_ref[...])


def _mem_attn(h1, g, wq_bf, qg, kn, v, wo_bf, bsz, seq):
    d = h1.shape[-1]
    m = kn.shape[1]
    tm = MEM_ROWS
    fix = lambda b, i: (0, 0)
    tok = pl.BlockSpec((None, tm, d), lambda b, i: (b, i, 0))
    per_b = pl.BlockSpec((None, m, d), lambda b, i: (b, 0, 0))
    full = lambda a: pl.BlockSpec(a.shape, fix)
    return pl.pallas_call(
        _mem_attn_kernel,
        grid=(bsz, seq // tm),
        in_specs=[tok, full(g), full(wq_bf), full(qg), per_b, per_b, full(wo_bf)],
        out_specs=tok,
        out_shape=jax.ShapeDtypeStruct((bsz, seq, d), F32),
        compiler_params=_cparams(("parallel", "parallel")),
        name="mem_attn",
    )(h1, g, wq_bf, qg, kn, v, wo_bf)


_CAND_ROWS = 80


def _top16_rows(scores):
    rows = scores.shape[0]
    ridx = lax.broadcasted_iota(jnp.int32, scores.shape, 0).astype(F32)
    work = scores
    rank = jnp.full(scores.shape, float(PEER_TOPK), F32)
    vals = []
    for r in range(PEER_TOPK):
        m = jnp.max(work, axis=0, keepdims=True)
        first = jnp.min(jnp.where(work == m, ridx, float(rows)), axis=0, keepdims=True)
        hit = ridx == first
        rank = jnp.where(hit, float(r), rank)
        work = jnp.where(hit, -jnp.inf, work)
        vals.append(m)
    return jnp.concatenate(vals, axis=0), rank


def _peer_select_kernel(ht_ref, g_ref, wq_ref, keys_ref, hn_ref, a0_ref, n0_ref, b1_ref, e1_ref):
    ht = ht_ref[...]
    ms = jnp.mean(ht * ht, axis=0, keepdims=True)
    hn = (ht * lax.rsqrt(ms + EPS) * g_ref[...]).astype(BF16)
    hn_ref[...] = hn
    qt = _dot(wq_ref[...], hn)
    nk = PEER_KEYS
    for h in range(PEER_HEADS):
        halves = []
        for s in range(2):
            r0 = (2 * h + s) * nk
            st = _dot(keys_ref[2 * h + s], qt[r0:r0 + nk, :].astype(BF16))
            sv, rank = _top16_rows(st)
            halves.append((st, sv, rank))
        (s0, sv0, rank0), (s1, sv1, rank1) = halves
        pieces = [sv0[0:1] + sv1]
        pieces += [sv0[a:a + 1] + sv1[0:8] for a in range(1, 8)]
        pieces += [sv0[8:16] + sv1[0:1]]
        cand = jnp.concatenate(pieces, axis=0)
        ridx = lax.broadcasted_iota(jnp.int32, cand.shape, 0).astype(F32)
        work = cand
        sel = jnp.zeros(cand.shape, F32)
        for _ in range(PEER_TOPK):
            m = jnp.max(work, axis=0, keepdims=True)
            first = jnp.min(jnp.where(work == m, ridx, float(_CAND_ROWS)), axis=0, keepdims=True)
            hit = ridx == first
            sel = jnp.where(hit, 1.0, sel)
            work = jnp.where(hit, -jnp.inf, work)
        z = jnp.sum(sel * jnp.exp(cand - cand[0:1]), axis=0, keepdims=True)
        counts = [jnp.sum(sel[0:16], axis=0, keepdims=True)]
        counts += [jnp.sum(sel[16 + 8 * (a - 1):24 + 8 * (a - 1)], axis=0, keepdims=True)
                   for a in range(1, 8)]
        counts += [sel[72 + a:73 + a] for a in range(8)]
        n_dense = jnp.zeros(rank0.shape, F32)
        for a in range(PEER_TOPK):
            n_dense = jnp.where(rank0 == float(a), counts[a], n_dense)
        a0_ref[h] = jnp.exp(s0 - sv0[0:1]) / z
        n0_ref[h] = n_dense
        b1_ref[h] = rank1
        e1_ref[h] = jnp.exp(s1 - sv1[0:1])


def _peer_select(h2t, g_col, wq_t_bf, keys_bf):
    d, n = h2t.shape
    tt = PEER_SEL_TOKENS
    col = lambda i: (0, i)
    dense = pl.BlockSpec((PEER_HEADS, PEER_KEYS, tt), lambda i: (0, 0, i))
    dense_shape = jax.ShapeDtypeStruct((PEER_HEADS, PEER_KEYS, n), F32)
    return pl.pallas_call(
        _peer_select_kernel,
        grid=(n // tt,),
        in_specs=[pl.BlockSpec((d, tt), col), pl.BlockSpec(g_col.shape, lambda i: (0, 0)),
                  pl.BlockSpec(wq_t_bf.shape, lambda i: (0, 0)),
                  pl.BlockSpec(keys_bf.shape, lambda i: (0, 0, 0))],
        out_specs=[pl.BlockSpec((d, tt), col)] + [dense] * 4,
        out_shape=[jax.ShapeDtypeStruct((d, n), BF16)] + [dense_shape] * 4,
        compiler_params=_cparams(("parallel",)),
        name="peer_select",
    )(h2t, g_col, wq_t_bf, keys_bf)


def _peer_dense_kernel(ht_ref, hn_ref, u_ref, vt_ref, a0_ref, n0_ref, b1_ref, e1_ref, o_ref, acc_ref):
    j = pl.program_id(1)

    @pl.when(j == 0)
    def _():
        acc_ref[...] = jnp.zeros_like(acc_ref)

    act = _gelu(_dot(u_ref[...], hn_ref[...]))
    nk = PEER_KEYS
    rows = []
    for i in range(PEER_EXPERT_TILE // nk):
        w = jnp.zeros((nk, act.shape[1]), F32)
        for h in range(PEER_HEADS):
            chosen = b1_ref[h] < n0_ref[h, i:i + 1, :]
            w = w + a0_ref[h, i:i + 1, :] * jnp.where(chosen, e1_ref[h], 0.0)
        rows.append((w * act[i * nk:(i + 1) * nk, :]).astype(BF16))
    acc_ref[...] += _dot(vt_ref[...], jnp.concatenate(rows, axis=0))

    @pl.when(j == pl.num_programs(1) - 1)
    def _():
        o_ref[...] = ht_ref[...] + acc_ref[...]


def _peer_dense(h2t, hn_t, u_bf, vt_bf, a0, n0, b1, e1):
    d, n = h2t.shape
    e = u_bf.shape[0]
    tt, te = PEER_TOKENS, PEER_EXPERT_TILE
    ipt = te // PEER_KEYS
    col = lambda i, j: (0, i)
    per_i = pl.BlockSpec((PEER_HEADS, ipt, tt), lambda i, j: (0, j, i))
    per_t = pl.BlockSpec((PEER_HEADS, PEER_KEYS, tt), lambda i, j: (0, 0, i))
    return pl.pallas_call(
        _peer_dense_kernel,
        grid=(n // tt, e // te),
        in_specs=[pl.BlockSpec((d, tt), col), pl.BlockSpec((d, tt), col),
                  pl.BlockSpec((te, d), lambda i, j: (j, 0)),
                  pl.BlockSpec((d, te), lambda i, j: (0, j)),
                  per_i, per_i, per_t, per_t],
        out_specs=pl.BlockSpec((d, tt), col),
        out_shape=jax.ShapeDtypeStruct((d, n), F32),
        scratch_shapes=[pltpu.VMEM((d, tt), F32)],
        compiler_params=_cparams(("parallel", "arbitrary")),
        name="peer_dense",
    )(h2t, hn_t, u_bf, vt_bf, a0, n0, b1, e1)


def _layer(h, mem, p):
    bsz, seq, d = h.shape
    n = bsz * seq
    x2 = h.reshape(n, d)
    row = lambda a: a.reshape(1, -1)

    seg = jnp.asarray(np.kron(np.eye(ATT_HEADS), np.ones((ATT_HEAD_DIM, ATT_HEAD_DIM))), BF16)
    q, k, v, u = _inproj(
        x2, row(p["norm_mix_g"]), p["w_in"].astype(BF16), seg,
        row(jnp.tile(p["att_q_g"], ATT_HEADS)),
        row(jnp.tile(p["att_k_g"], ATT_HEADS)))
    shp = (bsz, seq, ATT_WIDTH)
    att = _attention(q.reshape(shp), k.reshape(shp), v.reshape(shp),
                     _band_bias_tile(p["rel_bias"]), bsz, seq).reshape(n, ATT_WIDTH)

    t, g, c = SSM_T, SSM_GROUPS, SSM_GROUP
    chunks = seq // t
    n_steps = max(1, int(math.ceil(math.log2(chunks))))
    kflat, bcat, ccat, apr, api = _ssm_params(
        p["ssm_lam_re"], p["ssm_lam_im"], p["ssm_log_step"], p["ssm_b_re"], p["ssm_b_im"],
        p["ssm_c_re"], p["ssm_c_im"], n_steps)
    m_bf = _toeplitz(kflat).astype(BF16)
    u_g = u.reshape(bsz, chunks, t, g, c).transpose(3, 0, 1, 2, 4).reshape(g, bsz * chunks, t * c)
    d_flat = jnp.tile(p["ssm_d"], (1, t))[:, None, :]
    y_g = _ssm_main(u_g, m_bf, bcat, ccat, apr, api, d_flat, chunks, n_steps)
    y = y_g.reshape(g, bsz, chunks, t, c).transpose(1, 2, 3, 0, 4).reshape(n, SSM_WIDTH)

    w_out = p["w_out"].astype(BF16)
    h1 = _mix_out(x2, att, y, p["ssm_w_glu"].astype(BF16), row(p["ssm_b_glu"]),
                  row(p["att_out_g"]), row(p["ssm_out_g"]), w_out[:ATT_WIDTH], w_out[ATT_WIDTH:])

    kn, vm = _mem_kv(mem, row(p["norm_memkv_g"]), p["w_mem_k"].astype(BF16),
                     p["w_mem_v"].astype(BF16), row(p["mem_k_g"]))
    h2 = _mem_attn(h1.reshape(bsz, seq, d), row(p["norm_mem_g"]), p["w_mem_q"].astype(BF16),
                   row(p["mem_q_g"]), kn, vm, p["w_mem_o"].astype(BF16), bsz, seq)

    h2t = h2.reshape(n, d).T
    keys_bf = p["peer_keys"].reshape(PEER_HEADS * 2, PEER_KEYS, -1).astype(BF16)
    hn_t, a0, n0, b1, e1 = _peer_select(h2t, p["norm_peer_g"].reshape(d, 1),
                                        p["w_peer_q"].T.astype(BF16), keys_bf)
    h3t = _peer_dense(h2t, hn_t, p["peer_u"].astype(BF16), p["peer_v"].T.astype(BF16),
                      a0, n0, b1, e1)
    return h3t.T.reshape(bsz, seq, d)


_PARAM_NAMES = (
    "norm_mix_g", "w_in", "att_q_g", "att_k_g", "rel_bias", "ssm_lam_re", "ssm_lam_im",
    "ssm_log_step", "ssm_b_re", "ssm_b_im", "ssm_c_re", "ssm_c_im", "ssm_d", "ssm_w_glu",
    "ssm_b_glu", "att_out_g", "ssm_out_g", "w_out", "norm_mem_g", "norm_memkv_g", "w_mem_q",
    "w_mem_k", "w_mem_v", "mem_q_g", "mem_k_g", "w_mem_o", "norm_peer_g", "w_peer_q",
    "peer_keys", "peer_u", "peer_v")


def kernel(x, mem, norm_mix_g, w_in, att_q_g, att_k_g, rel_bias, ssm_lam_re, ssm_lam_im, ssm_log_step, ssm_b_re, ssm_b_im, ssm_c_re, ssm_c_im, ssm_d, ssm_w_glu, ssm_b_glu, att_out_g, ssm_out_g, w_out, norm_mem_g, norm_memkv_g, w_mem_q, w_mem_k, w_mem_v, mem_q_g, mem_k_g, w_mem_o, norm_peer_g, w_peer_q, peer_keys, peer_u, peer_v):
    stacked = (norm_mix_g, w_in, att_q_g, att_k_g, rel_bias, ssm_lam_re, ssm_lam_im, ssm_log_step,
               ssm_b_re, ssm_b_im, ssm_c_re, ssm_c_im, ssm_d, ssm_w_glu, ssm_b_glu, att_out_g,
               ssm_out_g, w_out, norm_mem_g, norm_memkv_g, w_mem_q, w_mem_k, w_mem_v, mem_q_g,
               mem_k_g, w_mem_o, norm_peer_g, w_peer_q, peer_keys, peer_u, peer_v)
    h = x
    for l in range(norm_mix_g.shape[0]):
        h = _layer(h, mem, {name: a[l] for name, a in zip(_PARAM_NAMES, stacked)})
    return h
```

```python
import functools
import math

import numpy as np
import jax
import jax.numpy as jnp
from jax import lax
from jax.experimental import pallas as pl
from jax.experimental.pallas import tpu as pltpu

F32 = jnp.float32
BF16 = jnp.bfloat16

EPS = 1e-6
NEG_INF = -1e30

CHUNK = 64
ATT_HEADS = 8
ATT_HEAD_DIM = 64
ATT_WIDTH = ATT_HEADS * ATT_HEAD_DIM
LEFT_CHUNKS = 8
REL_MAX = 128
SSM_GROUP = 16
SSM_GROUPS = 32
SSM_STATE = 64
SSM_WIDTH = SSM_GROUP * SSM_GROUPS
MEM_HEADS = 4
PEER_HEADS = 8
PEER_KEYS = 128
PEER_TOPK = 16

INPROJ_ROWS = 512
ATT_QBLOCK = 512
ATT_GROUP = 256
ATT_WINDOW = ATT_GROUP + LEFT_CHUNKS * CHUNK
SSM_T = 32
MIX_ROWS = 512
MEM_ROWS = 512
PEER_SEL_TOKENS = 256
PEER_TOKENS = 512
PEER_EXPERT_TILE = 1024
PEER_DENSE_CHUNK = 256
PEER_GATE_ROWS = 16

VMEM_LIMIT = 48 * 1024 * 1024


def _cparams(sem):
    return pltpu.CompilerParams(dimension_semantics=sem, vmem_limit_bytes=VMEM_LIMIT)


def _rms_rows(x, gain):
    ms = jnp.mean(x * x, axis=-1, keepdims=True)
    return x * lax.rsqrt(ms + EPS) * gain


def _dot(a, b):
    return jnp.dot(a, b, preferred_element_type=F32)


def _dot_nt(a, b):
    return lax.dot_general(a, b, (((1,), (1,)), ((), ())), preferred_element_type=F32)


def _inproj_kernel(x_ref, g_ref, w_ref, seg_ref, qg_ref, kg_ref, q_ref, k_ref, v_ref, u_ref):
    xn = _rms_rows(x_ref[...], g_ref[...])
    proj = _dot(xn.astype(BF16), w_ref[...])
    seg = seg_ref[...]

    def head_norm(z, gain):
        zz = z * z
        hi = zz.astype(BF16)
        lo = (zz - hi.astype(F32)).astype(BF16)
        ss = _dot(hi, seg) + _dot(lo, seg)
        return z * lax.rsqrt(ss * (1.0 / ATT_HEAD_DIM) + EPS) * gain

    q = head_norm(proj[:, :ATT_WIDTH], qg_ref[...]) * (ATT_HEAD_DIM ** -0.5)
    q_ref[...] = q.astype(BF16)
    k_ref[...] = head_norm(proj[:, ATT_WIDTH:2 * ATT_WIDTH], kg_ref[...]).astype(BF16)
    v_ref[...] = proj[:, 2 * ATT_WIDTH:3 * ATT_WIDTH].astype(BF16)
    u_ref[...] = proj[:, 3 * ATT_WIDTH:]


def _inproj(x2, g, w_in_bf, seg, qg, kg):
    n, d = x2.shape
    tm = INPROJ_ROWS
    row = lambda i: (i, 0)
    fix = lambda i: (0, 0)
    outw = ATT_WIDTH
    return pl.pallas_call(
        _inproj_kernel,
        grid=(n // tm,),
        in_specs=[pl.BlockSpec((tm, d), row), pl.BlockSpec((1, d), fix),
                  pl.BlockSpec(w_in_bf.shape, fix), pl.BlockSpec(seg.shape, fix),
                  pl.BlockSpec((1, outw), fix), pl.BlockSpec((1, outw), fix)],
        out_specs=[pl.BlockSpec((tm, outw), row)] * 3 + [pl.BlockSpec((tm, SSM_WIDTH), row)],
        out_shape=[jax.ShapeDtypeStruct((n, outw), BF16)] * 3
        + [jax.ShapeDtypeStruct((n, SSM_WIDTH), F32)],
        compiler_params=_cparams(("parallel",)),
        name="inproj",
    )(x2, g, w_in_bf, seg, qg, kg)


def _attn_kernel(q_ref, kp_ref, kc_ref, vp_ref, vc_ref, bias_ref, o_ref):
    blk = pl.program_id(1)
    kk = jnp.concatenate([kp_ref[...], kc_ref[...]], axis=0)
    vv = jnp.concatenate([vp_ref[...], vc_ref[...]], axis=0)
    chunks_per_block = ATT_QBLOCK // CHUNK
    key_chunk = lax.broadcasted_iota(jnp.int32, (1, ATT_WINDOW), 1) // CHUNK
    for gi in range(ATT_QBLOCK // ATT_GROUP):
        r0 = gi * ATT_GROUP
        q = q_ref[r0:r0 + ATT_GROUP, :]
        kwin = kk[r0:r0 + ATT_WINDOW, :]
        vwin = vv[r0:r0 + ATT_WINDOW, :]
        abs_chunk = blk * chunks_per_block + gi * (ATT_GROUP // CHUNK) + key_chunk - LEFT_CHUNKS
        valid = abs_chunk >= 0
        for h in range(ATT_HEADS):
            c0 = h * ATT_HEAD_DIM
            s = _dot_nt(q[:, c0:c0 + ATT_HEAD_DIM], kwin[:, c0:c0 + ATT_HEAD_DIM])
            s = jnp.where(valid, s + bias_ref[h], NEG_INF)
            m = jnp.max(s, axis=-1, keepdims=True)
            p = jnp.exp(s - m)
            l = jnp.sum(p, axis=-1, keepdims=True)
            o = _dot(p.astype(BF16), vwin[:, c0:c0 + ATT_HEAD_DIM])
            o_ref[r0:r0 + ATT_GROUP, c0:c0 + ATT_HEAD_DIM] = o / l


def _attention(q, k, v, bias, bsz, seq):
    qb = ATT_QBLOCK
    cur = lambda b, i: (b, i, 0)
    prev = lambda b, i: (b, jnp.maximum(i - 1, 0), 0)
    blk = (None, qb, ATT_WIDTH)
    return pl.pallas_call(
        _attn_kernel,
        grid=(bsz, seq // qb),
        in_specs=[pl.BlockSpec(blk, cur), pl.BlockSpec(blk, prev), pl.BlockSpec(blk, cur),
                  pl.BlockSpec(blk, prev), pl.BlockSpec(blk, cur),
                  pl.BlockSpec(bias.shape, lambda b, i: (0, 0, 0))],
        out_specs=pl.BlockSpec(blk, cur),
        out_shape=jax.ShapeDtypeStruct((bsz, seq, ATT_WIDTH), F32),
        compiler_params=_cparams(("parallel", "parallel")),
        name="attn",
    )(q, k, k, v, v, bias)


def _band_bias_tile(rel_bias):
    r = np.arange(ATT_GROUP)[:, None]
    s = np.arange(ATT_WINDOW)[None, :]
    dist = LEFT_CHUNKS * CHUNK + r - s
    bucket = np.clip(dist, -(CHUNK - 1), REL_MAX) + (CHUNK - 1)
    qc, kc = r // CHUNK, s // CHUNK
    band = (kc >= qc) & (kc <= qc + LEFT_CHUNKS)
    return jnp.where(band[None], rel_bias[:, bucket], NEG_INF)


def _ssm_param_kernel(lr_r_ref, li_r_ref, lr_c_ref, li_c_ref, ls_ref, taus_ref,
                      brt_ref, bit_ref, crt_ref, cit_ref,
                      k_ref, bcat_ref, ccat_ref, apr_ref, api_ref):
    t = SSM_T
    p = SSM_STATE
    step = jnp.exp(ls_ref[...])
    lr_r, li_r = lr_r_ref[...], li_r_ref[...]
    lr_c, li_c = lr_c_ref[...], li_c_ref[...]

    def zoh(lr, li):
        mag = jnp.exp(lr * step)
        ar = mag * jnp.cos(li * step)
        ai = mag * jnp.sin(li * step)
        zr = ar - 1.0
        den = lr * lr + li * li
        return (zr * lr + ai * li) / den, (ai * lr - zr * li) / den

    def power(lr, li, tau):
        mag = jnp.exp(lr * step * tau)
        ph = li * step * tau
        return mag * jnp.cos(ph), mag * jnp.sin(ph)

    fr_r, fi_r = zoh(lr_r, li_r)
    brt, bit = brt_ref[...], bit_ref[...]
    bbr_t = fr_r * brt - fi_r * bit
    bbi_t = fr_r * bit + fi_r * brt
    crt, cit = crt_ref[...], cit_ref[...]
    lane_t = (lax.broadcasted_iota(jnp.int32, (1, t * SSM_GROUP), 1) // SSM_GROUP).astype(F32)

    p0_r, p0_i = power(lr_c, li_c, lane_t)
    hp = lax.Precision.HIGHEST
    k_ref[...] = (jnp.dot(bbr_t, crt * p0_r - cit * p0_i, precision=hp, preferred_element_type=F32)
                  - jnp.dot(bbi_t, crt * p0_i + cit * p0_r, precision=hp, preferred_element_type=F32))

    tau = lax.broadcasted_iota(jnp.int32, (t, 1), 0).astype(F32)
    rv_r, rv_i = power(lr_r, li_r, float(t - 1) - tau)
    b_re = rv_r[:, None, :] * bbr_t[None] - rv_i[:, None, :] * bbi_t[None]
    b_im = rv_r[:, None, :] * bbi_t[None] + rv_i[:, None, :] * bbr_t[None]
    bcat_ref[:, 0:p] = b_re.reshape(t * SSM_GROUP, p)
    bcat_ref[:, p:2 * p] = b_im.reshape(t * SSM_GROUP, p)

    p1_r, p1_i = power(lr_c, li_c, lane_t + 1.0)
    ccat_ref[0:p, :] = crt * p1_r - cit * p1_i
    ccat_ref[p:2 * p, :] = -(crt * p1_i + cit * p1_r)

    dr, di = power(lr_r, li_r, taus_ref[...])
    apr_ref[:, 0:p] = dr
    apr_ref[:, p:2 * p] = dr
    api_ref[:, 0:p] = -di
    api_ref[:, p:2 * p] = di


def _ssm_params(lam_re, lam_im, log_step, b_re, b_im, c_re, c_im, n_steps):
    g, p, c = SSM_GROUPS, SSM_STATE, SSM_GROUP
    t = SSM_T
    taus = jnp.asarray([[float(t * 2 ** k)] for k in range(n_steps)], F32)
    c_re_t = jnp.swapaxes(c_re, 1, 2)
    c_im_t = jnp.swapaxes(c_im, 1, 2)
    args = [
        lam_re[:, None, :], lam_im[:, None, :], lam_re[:, :, None], lam_im[:, :, None],
        log_step[:, None, None], taus,
        jnp.swapaxes(b_re, 1, 2), jnp.swapaxes(b_im, 1, 2),
        jnp.tile(c_re_t, (1, 1, t)), jnp.tile(c_im_t, (1, 1, t)),
    ]

    def spec(a):
        if a.ndim == 2:
            return pl.BlockSpec(a.shape, lambda i: (0, 0))
        return pl.BlockSpec((None,) + a.shape[1:], lambda i: (i, 0, 0))

    out_dims = [(c, t * c), (t * c, 2 * p), (2 * p, t * c), (n_steps, 2 * p), (n_steps, 2 * p)]
    return pl.pallas_call(
        _ssm_param_kernel,
        grid=(g,),
        in_specs=[spec(a) for a in args],
        out_specs=[pl.BlockSpec((None,) + d, lambda i: (i, 0, 0)) for d in out_dims],
        out_shape=[jax.ShapeDtypeStruct((g,) + d, F32) for d in out_dims],
        compiler_params=_cparams(("parallel",)),
        name="ssm_params",
    )(*args)


def _ssm_main_kernel(u_ref, k_ref, bcat_ref, ccat_ref, apr_ref, api_ref, d_ref, y_ref, m_ref,
                     *, rows_per_batch, n_steps):
    k = k_ref[...]
    lane = lax.broadcasted_iota(jnp.int32, k.shape, 1)
    for s in range(SSM_T):
        sh = s * SSM_GROUP
        blk = k if s == 0 else jnp.where(lane >= sh, pltpu.roll(k, sh, 1), 0.0)
        m_ref[s * SSM_GROUP:(s + 1) * SSM_GROUP, :] = blk.astype(BF16)
    u = u_ref[...]
    ub = u.astype(BF16)
    y = _dot(ub, m_ref[...])
    x = _dot(ub, bcat_ref[...].astype(BF16))
    r = x.shape[0]
    k_in_batch = lax.broadcasted_iota(jnp.int32, (r, 1), 0) % rows_per_batch
    half = SSM_STATE
    for k in range(n_steps):
        d = 2 ** k
        sh = jnp.where(k_in_batch >= d, pltpu.roll(x, d, 0), 0.0)
        x = x + sh * apr_ref[k:k + 1, :] + pltpu.roll(sh, half, 1) * api_ref[k:k + 1, :]
    x_prev = jnp.where(k_in_batch >= 1, pltpu.roll(x, 1, 0), 0.0)
    y = y + _dot(x_prev.astype(BF16), ccat_ref[...].astype(BF16))
    y_ref[...] = y + u * d_ref[...]


def _ssm_main(u_g, k_lag, bcat, ccat, apr, api, d_flat, rows_per_batch, n_steps):
    g, r, w = u_g.shape
    per_g = lambda a: pl.BlockSpec((None,) + a.shape[1:], lambda i: (i, 0, 0))
    kern = functools.partial(_ssm_main_kernel, rows_per_batch=rows_per_batch, n_steps=n_steps)
    return pl.pallas_call(
        kern,
        grid=(g,),
        in_specs=[per_g(a) for a in (u_g, k_lag, bcat, ccat, apr, api, d_flat)],
        out_specs=per_g(u_g),
        out_shape=jax.ShapeDtypeStruct((g, r, w), F32),
        scratch_shapes=[pltpu.VMEM((w, w), BF16)],
        compiler_params=_cparams(("parallel",)),
        name="ssm_main",
    )(u_g, k_lag, bcat, ccat, apr, api, d_flat)


def _gelu(x):
    return 0.5 * x * (1.0 + lax.erf(x * (1.0 / math.sqrt(2.0))))


def _mix_out_kernel(x_ref, att_ref, y_ref, wg_ref, bg_ref, ag_ref, sg_ref, wa_ref, ws_ref, h_ref):
    y = _gelu(y_ref[...])
    z = _dot(y.astype(BF16), wg_ref[...]) + bg_ref[...]
    ssm = y * jax.nn.sigmoid(z)
    a = _rms_rows(att_ref[...], ag_ref[...])
    s = _rms_rows(ssm, sg_ref[...])
    h_ref[...] = x_ref[...] + _dot(a.astype(BF16), wa_ref[...]) + _dot(s.astype(BF16), ws_ref[...])


def _mix_out(x2, att, y, w_glu_bf, b_glu, att_g, ssm_g, w_out_a, w_out_s):
    n, d = x2.shape
    tm = MIX_ROWS
    row = lambda i: (i, 0)
    fix = lambda i: (0, 0)
    full = lambda a: pl.BlockSpec(a.shape, fix)
    return pl.pallas_call(
        _mix_out_kernel,
        grid=(n // tm,),
        in_specs=[pl.BlockSpec((tm, d), row), pl.BlockSpec((tm, ATT_WIDTH), row),
                  pl.BlockSpec((tm, SSM_WIDTH), row), full(w_glu_bf), full(b_glu), full(att_g),
                  full(ssm_g), full(w_out_a), full(w_out_s)],
        out_specs=pl.BlockSpec((tm, d), row),
        out_shape=jax.ShapeDtypeStruct((n, d), F32),
        compiler_params=_cparams(("parallel",)),
        name="mix_out",
    )(x2, att, y, w_glu_bf, b_glu, att_g, ssm_g, w_out_a, w_out_s)


def _mem_kv_kernel(mem_ref, g_ref, wk_ref, wv_ref, kg_ref, k_ref, v_ref):
    mn = _rms_rows(mem_ref[...], g_ref[...]).astype(BF16)
    k = _dot(mn, wk_ref[...])
    hd = k.shape[1] // MEM_HEADS
    for h in range(MEM_HEADS):
        k_ref[:, h * hd:(h + 1) * hd] = _rms_rows(k[:, h * hd:(h + 1) * hd], kg_ref[...]).astype(BF16)
    v_ref[...] = _dot(mn, wv_ref[...]).astype(BF16)


def _mem_kv(mem, g, wk_bf, wv_bf, kg):
    bsz, m, d = mem.shape
    fix = lambda b: (0, 0)
    per_b = pl.BlockSpec((None, m, d), lambda b: (b, 0, 0))
    return pl.pallas_call(
        _mem_kv_kernel,
        grid=(bsz,),
        in_specs=[per_b, pl.BlockSpec(g.shape, fix), pl.BlockSpec(wk_bf.shape, fix),
                  pl.BlockSpec(wv_bf.shape, fix), pl.BlockSpec(kg.shape, fix)],
        out_specs=[per_b, per_b],
        out_shape=[jax.ShapeDtypeStruct((bsz, m, d), BF16)] * 2,
        compiler_params=_cparams(("parallel",)),
        name="mem_kv",
    )(mem, g, wk_bf, wv_bf, kg)


def _mem_attn_kernel(h_ref, g_ref, wq_ref, qg_ref, k_ref, v_ref, wo_ref, o_ref):
    h1 = h_ref[...]
    hn = _rms_rows(h1, g_ref[...]).astype(BF16)
    q = _dot(hn, wq_ref[...])
    hd = q.shape[1] // MEM_HEADS
    outs = []
    for h in range(MEM_HEADS):
        sl = slice(h * hd, (h + 1) * hd)
        qh = (_rms_rows(q[:, sl], qg_ref[...]) * (hd ** -0.5)).astype(BF16)
        s = _dot_nt(qh, k_ref[:, sl])
        m = jnp.max(s, axis=-1, keepdims=True)
        p = jnp.exp(s - m)
        l = jnp.sum(p, axis=-1, keepdims=True)
        outs.append(_dot(p.astype(BF16), v_ref[:, sl]) / l)
    o = jnp.concatenate(outs, axis=-1).astype(BF16)
    o_ref[...] = (h1 + _dot(o, wo_ref[...])).T


def _mem_attn(h1, g, wq_bf, qg, kn, v, wo_bf, bsz, seq):
    d = h1.shape[-1]
    m = kn.shape[1]
    tm = MEM_ROWS
    steps = seq // tm
    fix = lambda b, i: (0, 0)
    tok = pl.BlockSpec((None, tm, d), lambda b, i: (b, i, 0))
    per_b = pl.BlockSpec((None, m, d), lambda b, i: (b, 0, 0))
    full = lambda a: pl.BlockSpec(a.shape, fix)
    return pl.pallas_call(
        _mem_attn_kernel,
        grid=(bsz, steps),
        in_specs=[tok, full(g), full(wq_bf), full(qg), per_b, per_b, full(wo_bf)],
        out_specs=pl.BlockSpec((d, tm), lambda b, i: (0, b * steps + i)),
        out_shape=jax.ShapeDtypeStruct((d, bsz * seq), F32),
        compiler_params=_cparams(("parallel", "parallel")),
        name="mem_attn",
    )(h1, g, wq_bf, qg, kn, v, wo_bf)


_CAND_ROWS = 80


def _top16_rows(scores, exact):
    rows = scores.shape[0]
    ridx = lax.broadcasted_iota(jnp.int32, scores.shape, 0).astype(F32) if exact else None
    work = scores
    rank = jnp.full(scores.shape, float(PEER_TOPK), F32)
    vals = []
    for r in range(PEER_TOPK):
        m = jnp.max(work, axis=0, keepdims=True)
        hit = work == m
        if exact:
            first = jnp.min(jnp.where(hit, ridx, float(rows)), axis=0, keepdims=True)
            hit = ridx == first
        rank = jnp.where(hit, float(r), rank)
        work = jnp.where(hit, -jnp.inf, work)
        vals.append(m)
    ranked = jnp.sum(jnp.where(rank < float(PEER_TOPK), 1.0, 0.0), axis=0, keepdims=True)
    return jnp.concatenate(vals, axis=0), rank, ranked


def _peer_head(keys_ref, qt_ref, h, exact):
    nk = PEER_KEYS
    halves = []
    for s in range(2):
        r0 = (2 * h + s) * nk
        st = _dot(keys_ref[2 * h + s], qt_ref[r0:r0 + nk, :].astype(BF16))
        halves.append((st,) + _top16_rows(st, exact))
    (s0, sv0, rank0, ranked0), (s1, sv1, rank1, ranked1) = halves
    pieces = [sv0[0:1] + sv1]
    pieces += [sv0[a:a + 1] + sv1[0:8] for a in range(1, 8)]
    pieces += [sv0[8:16] + sv1[0:1]]
    cand = jnp.concatenate(pieces, axis=0)
    ridx = lax.broadcasted_iota(jnp.int32, cand.shape, 0).astype(F32) if exact else None
    work = cand
    for _ in range(PEER_TOPK):
        m = jnp.max(work, axis=0, keepdims=True)
        hit = work == m
        if exact:
            first = jnp.min(jnp.where(hit, ridx, float(_CAND_ROWS)), axis=0, keepdims=True)
            hit = ridx == first
        work = jnp.where(hit, -jnp.inf, work)
    sel = jnp.where(work == -jnp.inf, 1.0, 0.0)
    chosen = jnp.sum(sel, axis=0, keepdims=True)
    z = jnp.sum(sel * jnp.exp(cand - cand[0:1]), axis=0, keepdims=True)
    counts = [jnp.sum(sel[0:16], axis=0, keepdims=True)]
    counts += [jnp.sum(sel[16 + 8 * (a - 1):24 + 8 * (a - 1)], axis=0, keepdims=True)
               for a in range(1, 8)]
    counts += [sel[72 + a:73 + a] for a in range(8)]
    n_dense = jnp.zeros(rank0.shape, F32)
    for a in range(PEER_TOPK):
        n_dense = jnp.where(rank0 == float(a), counts[a], n_dense)
    a_dense = jnp.exp(s0 - sv0[0:1]) / z
    e_dense = jnp.exp(s1 - sv1[0:1])
    k = float(PEER_TOPK)
    clean = (ranked0 == k) & (ranked1 == k) & (chosen == k)
    return a_dense, n_dense, rank1, e_dense, clean


def _peer_select_kernel(ht_ref, g_ref, wq_ref, keys_ref, hn_ref, a0_ref, n0_ref, b1_ref, e1_ref,
                        qt_ref):
    ht = ht_ref[...]
    ms = jnp.mean(ht * ht, axis=0, keepdims=True)
    hn = (ht * lax.rsqrt(ms + EPS) * g_ref[...]).astype(BF16)
    hn_ref[...] = hn
    qt_ref[...] = _dot(wq_ref[...], hn)

    def write(h, exact):
        a_dense, n_dense, rank1, e_dense, clean = _peer_head(keys_ref, qt_ref, h, exact)
        a0_ref[h] = a_dense[:, None, :]
        n0_ref[h] = n_dense[:, None, :]
        b1_ref[h] = rank1.astype(BF16)
        e1_ref[h] = e_dense.astype(BF16)
        return clean

    tied = jnp.zeros((1, ht.shape[1]), F32)
    for h in range(PEER_HEADS):
        tied = jnp.maximum(tied, jnp.where(write(h, exact=False), 0.0, 1.0))

    @pl.when(jnp.max(tied) > 0.0)
    def _():
        for h in range(PEER_HEADS):
            write(h, exact=True)


def _peer_select(h2t, g_col, wq_t_bf, keys_bf):
    d, n = h2t.shape
    tt = PEER_SEL_TOKENS
    col = lambda i: (0, i)
    dense = pl.BlockSpec((PEER_HEADS, PEER_KEYS, tt), lambda i: (0, 0, i))
    dense_shape = jax.ShapeDtypeStruct((PEER_HEADS, PEER_KEYS, n), BF16)
    per_key = pl.BlockSpec((PEER_HEADS, PEER_KEYS, 1, tt), lambda i: (0, 0, 0, i))
    per_key_shape = jax.ShapeDtypeStruct((PEER_HEADS, PEER_KEYS, 1, n), F32)
    return pl.pallas_call(
        _peer_select_kernel,
        grid=(n // tt,),
        in_specs=[pl.BlockSpec((d, tt), col), pl.BlockSpec(g_col.shape, lambda i: (0, 0)),
                  pl.BlockSpec(wq_t_bf.shape, lambda i: (0, 0)),
                  pl.BlockSpec(keys_bf.shape, lambda i: (0, 0, 0))],
        out_specs=[pl.BlockSpec((d, tt), col), per_key, per_key, dense, dense],
        out_shape=[jax.ShapeDtypeStruct((d, n), BF16), per_key_shape, per_key_shape, dense_shape,
                   dense_shape],
        scratch_shapes=[pltpu.VMEM((wq_t_bf.shape[0], tt), F32)],
        compiler_params=_cparams(("parallel",)),
        name="peer_select",
    )(h2t, g_col, wq_t_bf, keys_bf)


def _peer_dense_kernel(ht_ref, hn_ref, u_ref, vt_ref, a0_ref, n0_ref, b1_ref, e1_ref, o_ref,
                       acc_ref, act0_ref, act1_ref, y0_ref, y1_ref):
    j = pl.program_id(1)

    @pl.when(j == 0)
    def _():
        acc_ref[...] = jnp.zeros_like(acc_ref)

    nk = PEER_KEYS
    kc = PEER_DENSE_CHUNK
    n_chunks = PEER_EXPERT_TILE // kc
    act_refs = (act0_ref, act1_ref)
    y_refs = (y0_ref, y1_ref)

    def activations(c):
        act_refs[c % 2][...] = _dot(u_ref[c * kc:(c + 1) * kc, :], hn_ref[...])

    def gate(c):
        act_ref, y_ref = act_refs[c % 2], y_refs[c % 2]
        gr = PEER_GATE_ROWS
        tokens = act_ref.shape[1]
        for ii in range(kc // nk):
            i = c * (kc // nk) + ii
            w = [None] * (nk // gr)
            for h in range(PEER_HEADS):
                n_i = jnp.broadcast_to(n0_ref[h, i], (gr, tokens)).astype(BF16)
                a_i = jnp.broadcast_to(a0_ref[h, i], (gr, tokens)).astype(BF16)
                for jb in range(nk // gr):
                    js = slice(jb * gr, (jb + 1) * gr)
                    term = a_i * jnp.where(b1_ref[h, js, :] < n_i, e1_ref[h, js, :], 0.0)
                    w[jb] = term if w[jb] is None else w[jb] + term
            for jb in range(nk // gr):
                rows = slice(ii * nk + jb * gr, ii * nk + (jb + 1) * gr)
                y_ref[rows, :] = w[jb] * _gelu(act_ref[rows, :]).astype(BF16)

    def project(c):
        acc_ref[...] += _dot(vt_ref[:, c * kc:(c + 1) * kc], y_refs[c % 2][...])

    activations(0)
    for c in range(n_chunks):
        if c + 1 < n_chunks:
            activations(c + 1)
        gate(c)
        if c >= 1:
            project(c - 1)
    project(n_chunks - 1)

    @pl.when(j == pl.num_programs(1) - 1)
    def _():
        o_ref[...] = (ht_ref[...] + acc_ref[...]).T


def _peer_dense(h2t, hn_t, u_bf, vt_bf, a0, n0, b1, e1):
    d, n = h2t.shape
    e = u_bf.shape[0]
    tt, te = PEER_TOKENS, PEER_EXPERT_TILE
    ipt = te // PEER_KEYS
    col = lambda i, j: (0, i)
    per_i = pl.BlockSpec((PEER_HEADS, ipt, 1, tt), lambda i, j: (0, j, 0, i))
    per_t = pl.BlockSpec((PEER_HEADS, PEER_KEYS, tt), lambda i, j: (0, 0, i))
    return pl.pallas_call(
        _peer_dense_kernel,
        grid=(n // tt, e // te),
        in_specs=[pl.BlockSpec((d, tt), col), pl.BlockSpec((d, tt), col),
                  pl.BlockSpec((te, d), lambda i, j: (j, 0)),
                  pl.BlockSpec((d, te), lambda i, j: (0, j)),
                  per_i, per_i, per_t, per_t],
        out_specs=pl.BlockSpec((tt, d), lambda i, j: (i, 0)),
        out_shape=jax.ShapeDtypeStruct((n, d), F32),
        scratch_shapes=[pltpu.VMEM((d, tt), F32)]
        + [pltpu.VMEM((PEER_DENSE_CHUNK, tt), F32)] * 2 + [pltpu.VMEM((PEER_DENSE_CHUNK, tt), BF16)] * 2,
        compiler_params=_cparams(("parallel", "arbitrary")),
        name="peer_dense",
    )(h2t, hn_t, u_bf, vt_bf, a0, n0, b1, e1)


def _layer(h, mem, p):
    bsz, seq, d = h.shape
    n = bsz * seq
    x2 = h.reshape(n, d)
    row = lambda a: a.reshape(1, -1)

    seg = jnp.asarray(np.kron(np.eye(ATT_HEADS), np.ones((ATT_HEAD_DIM, ATT_HEAD_DIM))), BF16)
    q, k, v, u = _inproj(
        x2, row(p["norm_mix_g"]), p["w_in"].astype(BF16), seg,
        row(jnp.tile(p["att_q_g"], ATT_HEADS)),
        row(jnp.tile(p["att_k_g"], ATT_HEADS)))
    shp = (bsz, seq, ATT_WIDTH)
    att = _attention(q.reshape(shp), k.reshape(shp), v.reshape(shp),
                     _band_bias_tile(p["rel_bias"]), bsz, seq).reshape(n, ATT_WIDTH)

    t, g, c = SSM_T, SSM_GROUPS, SSM_GROUP
    chunks = seq // t
    n_steps = max(1, int(math.ceil(math.log2(chunks))))
    k_lag, bcat, ccat, apr, api = _ssm_params(
        p["ssm_lam_re"], p["ssm_lam_im"], p["ssm_log_step"], p["ssm_b_re"], p["ssm_b_im"],
        p["ssm_c_re"], p["ssm_c_im"], n_steps)
    u_g = u.reshape(bsz, chunks, t, g, c).transpose(3, 0, 1, 2, 4).reshape(g, bsz * chunks, t * c)
    d_flat = jnp.tile(p["ssm_d"], (1, t))[:, None, :]
    y_g = _ssm_main(u_g, k_lag, bcat, ccat, apr, api, d_flat, chunks, n_steps)
    y = y_g.reshape(g, bsz, chunks, t, c).transpose(1, 2, 3, 0, 4).reshape(n, SSM_WIDTH)

    w_out = p["w_out"].astype(BF16)
    h1 = _mix_out(x2, att, y, p["ssm_w_glu"].astype(BF16), row(p["ssm_b_glu"]),
                  row(p["att_out_g"]), row(p["ssm_out_g"]), w_out[:ATT_WIDTH], w_out[ATT_WIDTH:])

    kn, vm = _mem_kv(mem, row(p["norm_memkv_g"]), p["w_mem_k"].astype(BF16),
                     p["w_mem_v"].astype(BF16), row(p["mem_k_g"]))
    h2t = _mem_attn(h1.reshape(bsz, seq, d), row(p["norm_mem_g"]), p["w_mem_q"].astype(BF16),
                    row(p["mem_q_g"]), kn, vm, p["w_mem_o"].astype(BF16), bsz, seq)

    keys_bf = p["peer_keys"].reshape(PEER_HEADS * 2, PEER_KEYS, -1).astype(BF16)
    hn_t, a0, n0, b1, e1 = _peer_select(h2t, p["norm_peer_g"].reshape(d, 1),
                                        p["w_peer_q"].T.astype(BF16), keys_bf)
    h3 = _peer_dense(h2t, hn_t, p["peer_u"].astype(BF16), p["peer_v"].T.astype(BF16),
                     a0, n0, b1, e1)
    return h3.reshape(bsz, seq, d)


_PARAM_NAMES = (
    "norm_mix_g", "w_in", "att_q_g", "att_k_g", "rel_bias", "ssm_lam_re", "ssm_lam_im",
    "ssm_log_step", "ssm_b_re", "ssm_b_im", "ssm_c_re", "ssm_c_im", "ssm_d", "ssm_w_glu",
    "ssm_b_glu", "att_out_g", "ssm_out_g", "w_out", "norm_mem_g", "norm_memkv_g", "w_mem_q",
    "w_mem_k", "w_mem_v", "mem_q_g", "mem_k_g", "w_mem_o", "norm_peer_g", "w_peer_q",
    "peer_keys", "peer_u", "peer_v")


def kernel(x, mem, norm_mix_g, w_in, att_q_g, att_k_g, rel_bias, ssm_lam_re, ssm_lam_im, ssm_log_step, ssm_b_re, ssm_b_im, ssm_c_re, ssm_c_im, ssm_d, ssm_w_glu, ssm_b_glu, att_out_g, ssm_out_g, w_out, norm_mem_g, norm_memkv_g, w_mem_q, w_mem_k, w_mem_v, mem_q_g, mem_k_g, w_mem_o, norm_peer_g, w_peer_q, peer_keys, peer_u, peer_v):
    stacked = (norm_mix_g, w_in, att_q_g, att_k_g, rel_bias, ssm_lam_re, ssm_lam_im, ssm_log_step,
               ssm_b_re, ssm_b_im, ssm_c_re, ssm_c_im, ssm_d, ssm_w_glu, ssm_b_glu, att_out_g,
               ssm_out_g, w_out, norm_mem_g, norm_memkv_g, w_mem_q, w_mem_k, w_mem_v, mem_q_g,
               mem_k_g, w_mem_o, norm_peer_g, w_peer_q, peer_keys, peer_u, peer_v)
    h = x
    for l in range(norm_mix_g.shape[0]):
        h = _layer(h, mem, {name: a[l] for name, a in zip(_PARAM_NAMES, stacked)})
    return h
```

```python
import functools
import math

import numpy as np
import jax
import jax.numpy as jnp
from jax import lax
from jax.experimental import pallas as pl
from jax.experimental.pallas import tpu as pltpu

F32 = jnp.float32
BF16 = jnp.bfloat16

EPS = 1e-6
NEG_INF = -1e30

CHUNK = 64
ATT_HEADS = 8
ATT_HEAD_DIM = 64
ATT_WIDTH = ATT_HEADS * ATT_HEAD_DIM
LEFT_CHUNKS = 8
REL_MAX = 128
SSM_GROUP = 16
SSM_GROUPS = 32
SSM_STATE = 64
SSM_WIDTH = SSM_GROUP * SSM_GROUPS
MEM_HEADS = 4
PEER_HEADS = 8
PEER_KEYS = 128
PEER_TOPK = 16

INPROJ_ROWS = 512
ATT_QBLOCK = 512
ATT_GROUP = 256
ATT_WINDOW = ATT_GROUP + LEFT_CHUNKS * CHUNK
SSM_T = 32
MIX_ROWS = 512
MEM_ROWS = 512
PEER_SEL_TOKENS = 256
PEER_TOKENS = 512
PEER_EXPERT_TILE = 1024
PEER_DENSE_CHUNK = 256
PEER_GATE_ROWS = 16

VMEM_LIMIT = 48 * 1024 * 1024


def _cparams(sem):
    return pltpu.CompilerParams(dimension_semantics=sem, vmem_limit_bytes=VMEM_LIMIT)


def _rms_rows(x, gain):
    ms = jnp.mean(x * x, axis=-1, keepdims=True)
    return x * lax.rsqrt(ms + EPS) * gain


def _dot(a, b):
    return jnp.dot(a, b, preferred_element_type=F32)


def _dot_nt(a, b):
    return lax.dot_general(a, b, (((1,), (1,)), ((), ())), preferred_element_type=F32)


def _inproj_kernel(x_ref, g_ref, w_ref, seg_ref, qg_ref, kg_ref, q_ref, k_ref, v_ref, u_ref):
    xn = _rms_rows(x_ref[...], g_ref[...])
    proj = _dot(xn.astype(BF16), w_ref[...])
    seg = seg_ref[...]

    def head_norm(z, gain):
        zz = z * z
        hi = zz.astype(BF16)
        lo = (zz - hi.astype(F32)).astype(BF16)
        ss = _dot(hi, seg) + _dot(lo, seg)
        return z * lax.rsqrt(ss * (1.0 / ATT_HEAD_DIM) + EPS) * gain

    q = head_norm(proj[:, :ATT_WIDTH], qg_ref[...]) * (ATT_HEAD_DIM ** -0.5)
    q_ref[...] = q.astype(BF16)
    k_ref[...] = head_norm(proj[:, ATT_WIDTH:2 * ATT_WIDTH], kg_ref[...]).astype(BF16)
    v_ref[...] = proj[:, 2 * ATT_WIDTH:3 * ATT_WIDTH].astype(BF16)
    u_ref[...] = proj[:, 3 * ATT_WIDTH:]


def _inproj(x2, g, w_in_bf, seg, qg, kg):
    n, d = x2.shape
    tm = INPROJ_ROWS
    row = lambda i: (i, 0)
    fix = lambda i: (0, 0)
    outw = ATT_WIDTH
    return pl.pallas_call(
        _inproj_kernel,
        grid=(n // tm,),
        in_specs=[pl.BlockSpec((tm, d), row), pl.BlockSpec((1, d), fix),
                  pl.BlockSpec(w_in_bf.shape, fix), pl.BlockSpec(seg.shape, fix),
                  pl.BlockSpec((1, outw), fix), pl.BlockSpec((1, outw), fix)],
        out_specs=[pl.BlockSpec((tm, outw), row)] * 3 + [pl.BlockSpec((tm, SSM_WIDTH), row)],
        out_shape=[jax.ShapeDtypeStruct((n, outw), BF16)] * 3
        + [jax.ShapeDtypeStruct((n, SSM_WIDTH), F32)],
        compiler_params=_cparams(("parallel",)),
        name="inproj",
    )(x2, g, w_in_bf, seg, qg, kg)


def _attn_kernel(q_ref, kp_ref, kc_ref, vp_ref, vc_ref, bias_ref, o_ref):
    blk = pl.program_id(1)
    kk = jnp.concatenate([kp_ref[...], kc_ref[...]], axis=0)
    vv = jnp.concatenate([vp_ref[...], vc_ref[...]], axis=0)
    chunks_per_block = ATT_QBLOCK // CHUNK
    key_chunk = lax.broadcasted_iota(jnp.int32, (1, ATT_WINDOW), 1) // CHUNK
    for gi in range(ATT_QBLOCK // ATT_GROUP):
        r0 = gi * ATT_GROUP
        q = q_ref[r0:r0 + ATT_GROUP, :]
        kwin = kk[r0:r0 + ATT_WINDOW, :]
        vwin = vv[r0:r0 + ATT_WINDOW, :]
        abs_chunk = blk * chunks_per_block + gi * (ATT_GROUP // CHUNK) + key_chunk - LEFT_CHUNKS
        valid = abs_chunk >= 0
        for h in range(ATT_HEADS):
            c0 = h * ATT_HEAD_DIM
            s = _dot_nt(q[:, c0:c0 + ATT_HEAD_DIM], kwin[:, c0:c0 + ATT_HEAD_DIM])
            s = jnp.where(valid, s + bias_ref[h], NEG_INF)
            m = jnp.max(s, axis=-1, keepdims=True)
            p = jnp.exp(s - m)
            l = jnp.sum(p, axis=-1, keepdims=True)
            o = _dot(p.astype(BF16), vwin[:, c0:c0 + ATT_HEAD_DIM])
            o_ref[r0:r0 + ATT_GROUP, c0:c0 + ATT_HEAD_DIM] = o / l


def _attention(q, k, v, bias, bsz, seq):
    qb = ATT_QBLOCK
    cur = lambda b, i: (b, i, 0)
    prev = lambda b, i: (b, jnp.maximum(i - 1, 0), 0)
    blk = (None, qb, ATT_WIDTH)
    return pl.pallas_call(
        _attn_kernel,
        grid=(bsz, seq // qb),
        in_specs=[pl.BlockSpec(blk, cur), pl.BlockSpec(blk, prev), pl.BlockSpec(blk, cur),
                  pl.BlockSpec(blk, prev), pl.BlockSpec(blk, cur),
                  pl.BlockSpec(bias.shape, lambda b, i: (0, 0, 0))],
        out_specs=pl.BlockSpec(blk, cur),
        out_shape=jax.ShapeDtypeStruct((bsz, seq, ATT_WIDTH), F32),
        compiler_params=_cparams(("parallel", "parallel")),
        name="attn",
    )(q, k, k, v, v, bias)


def _band_bias_tile(rel_bias):
    heads = rel_bias.shape[0]
    period = 1024
    delta = (np.arange(period) + (ATT_GROUP - 1)) % period - (ATT_GROUP - 1)
    bucket = np.clip(LEFT_CHUNKS * CHUNK - delta, -(CHUNK - 1), REL_MAX) + (CHUNK - 1)
    vec = rel_bias[:, bucket]
    flat = jnp.tile(vec, (1, ATT_GROUP))[:, :ATT_GROUP * (period - 1)]
    tile = flat.reshape(heads, ATT_GROUP, period - 1)[:, :, :ATT_WINDOW]
    r = np.arange(ATT_GROUP)[:, None]
    s = np.arange(ATT_WINDOW)[None, :]
    qc, kc = r // CHUNK, s // CHUNK
    band = (kc >= qc) & (kc <= qc + LEFT_CHUNKS)
    return jnp.where(band[None], tile, NEG_INF)


def _ssm_param_kernel(lr_r_ref, li_r_ref, lr_c_ref, li_c_ref, ls_ref, taus_ref,
                      brt_ref, bit_ref, crt_ref, cit_ref,
                      k_ref, bcat_ref, ccat_ref, apr_ref, api_ref):
    t = SSM_T
    p = SSM_STATE
    step = jnp.exp(ls_ref[...])
    lr_r, li_r = lr_r_ref[...], li_r_ref[...]
    lr_c, li_c = lr_c_ref[...], li_c_ref[...]

    def zoh(lr, li):
        mag = jnp.exp(lr * step)
        ar = mag * jnp.cos(li * step)
        ai = mag * jnp.sin(li * step)
        zr = ar - 1.0
        den = lr * lr + li * li
        return (zr * lr + ai * li) / den, (ai * lr - zr * li) / den

    def power(lr, li, tau):
        mag = jnp.exp(lr * step * tau)
        ph = li * step * tau
        return mag * jnp.cos(ph), mag * jnp.sin(ph)

    fr_r, fi_r = zoh(lr_r, li_r)
    brt, bit = brt_ref[...], bit_ref[...]
    bbr_t = fr_r * brt - fi_r * bit
    bbi_t = fr_r * bit + fi_r * brt
    crt, cit = crt_ref[...], cit_ref[...]
    lane_t = (lax.broadcasted_iota(jnp.int32, (1, t * SSM_GROUP), 1) // SSM_GROUP).astype(F32)

    p0_r, p0_i = power(lr_c, li_c, lane_t)
    hp = lax.Precision.HIGHEST
    k_ref[...] = (jnp.dot(bbr_t, crt * p0_r - cit * p0_i, precision=hp, preferred_element_type=F32)
                  - jnp.dot(bbi_t, crt * p0_i + cit * p0_r, precision=hp, preferred_element_type=F32))

    tau = lax.broadcasted_iota(jnp.int32, (t, 1), 0).astype(F32)
    rv_r, rv_i = power(lr_r, li_r, float(t - 1) - tau)
    b_re = rv_r[:, None, :] * bbr_t[None] - rv_i[:, None, :] * bbi_t[None]
    b_im = rv_r[:, None, :] * bbi_t[None] + rv_i[:, None, :] * bbr_t[None]
    bcat_ref[:, 0:p] = b_re.reshape(t * SSM_GROUP, p)
    bcat_ref[:, p:2 * p] = b_im.reshape(t * SSM_GROUP, p)

    p1_r, p1_i = power(lr_c, li_c, lane_t + 1.0)
    ccat_ref[0:p, :] = crt * p1_r - cit * p1_i
    ccat_ref[p:2 * p, :] = -(crt * p1_i + cit * p1_r)

    dr, di = power(lr_r, li_r, taus_ref[...])
    apr_ref[:, 0:p] = dr
    apr_ref[:, p:2 * p] = dr
    api_ref[:, 0:p] = -di
    api_ref[:, p:2 * p] = di


def _ssm_params(lam_re, lam_im, log_step, b_re, b_im, c_re, c_im, n_steps):
    g, p, c = SSM_GROUPS, SSM_STATE, SSM_GROUP
    t = SSM_T
    taus = jnp.asarray([[float(t * 2 ** k)] for k in range(n_steps)], F32)
    c_re_t = jnp.swapaxes(c_re, 1, 2)
    c_im_t = jnp.swapaxes(c_im, 1, 2)
    args = [
        lam_re[:, None, :], lam_im[:, None, :], lam_re[:, :, None], lam_im[:, :, None],
        log_step[:, None, None], taus,
        jnp.swapaxes(b_re, 1, 2), jnp.swapaxes(b_im, 1, 2),
        jnp.tile(c_re_t, (1, 1, t)), jnp.tile(c_im_t, (1, 1, t)),
    ]

    def spec(a):
        if a.ndim == 2:
            return pl.BlockSpec(a.shape, lambda i: (0, 0))
        return pl.BlockSpec((None,) + a.shape[1:], lambda i: (i, 0, 0))

    out_dims = [(c, t * c), (t * c, 2 * p), (2 * p, t * c), (n_steps, 2 * p), (n_steps, 2 * p)]
    return pl.pallas_call(
        _ssm_param_kernel,
        grid=(g,),
        in_specs=[spec(a) for a in args],
        out_specs=[pl.BlockSpec((None,) + d, lambda i: (i, 0, 0)) for d in out_dims],
        out_shape=[jax.ShapeDtypeStruct((g,) + d, F32) for d in out_dims],
        compiler_params=_cparams(("parallel",)),
        name="ssm_params",
    )(*args)


def _ssm_main_kernel(u_ref, k_ref, bcat_ref, ccat_ref, apr_ref, api_ref, d_ref, y_ref, m_ref,
                     *, rows_per_batch, n_steps):
    k = k_ref[...]
    lane = lax.broadcasted_iota(jnp.int32, k.shape, 1)
    for s in range(SSM_T):
        sh = s * SSM_GROUP
        blk = k if s == 0 else jnp.where(lane >= sh, pltpu.roll(k, sh, 1), 0.0)
        m_ref[s * SSM_GROUP:(s + 1) * SSM_GROUP, :] = blk.astype(BF16)
    u = u_ref[...]
    ub = u.astype(BF16)
    y = _dot(ub, m_ref[...])
    x = _dot(ub, bcat_ref[...].astype(BF16))
    r = x.shape[0]
    k_in_batch = lax.broadcasted_iota(jnp.int32, (r, 1), 0) % rows_per_batch
    half = SSM_STATE
    for k in range(n_steps):
        d = 2 ** k
        sh = jnp.where(k_in_batch >= d, pltpu.roll(x, d, 0), 0.0)
        x = x + sh * apr_ref[k:k + 1, :] + pltpu.roll(sh, half, 1) * api_ref[k:k + 1, :]
    x_prev = jnp.where(k_in_batch >= 1, pltpu.roll(x, 1, 0), 0.0)
    y = y + _dot(x_prev.astype(BF16), ccat_ref[...].astype(BF16))
    y_ref[...] = y + u * d_ref[...]


def _ssm_main(u_g, k_lag, bcat, ccat, apr, api, d_flat, rows_per_batch, n_steps):
    g, r, w = u_g.shape
    per_g = lambda a: pl.BlockSpec((None,) + a.shape[1:], lambda i: (i, 0, 0))
    kern = functools.partial(_ssm_main_kernel, rows_per_batch=rows_per_batch, n_steps=n_steps)
    return pl.pallas_call(
        kern,
        grid=(g,),
        in_specs=[per_g(a) for a in (u_g, k_lag, bcat, ccat, apr, api, d_flat)],
        out_specs=per_g(u_g),
        out_shape=jax.ShapeDtypeStruct((g, r, w), F32),
        scratch_shapes=[pltpu.VMEM((w, w), BF16)],
        compiler_params=_cparams(("parallel",)),
        name="ssm_main",
    )(u_g, k_lag, bcat, ccat, apr, api, d_flat)


def _gelu(x):
    return 0.5 * x * (1.0 + lax.erf(x * (1.0 / math.sqrt(2.0))))


def _mix_out_kernel(x_ref, att_ref, y_ref, wg_ref, bg_ref, ag_ref, sg_ref, wa_ref, ws_ref, h_ref):
    y = _gelu(y_ref[...])
    z = _dot(y.astype(BF16), wg_ref[...]) + bg_ref[...]
    ssm = y * jax.nn.sigmoid(z)
    a = _rms_rows(att_ref[...], ag_ref[...])
    s = _rms_rows(ssm, sg_ref[...])
    h_ref[...] = x_ref[...] + _dot(a.astype(BF16), wa_ref[...]) + _dot(s.astype(BF16), ws_ref[...])


def _mix_out(x2, att, y, w_glu_bf, b_glu, att_g, ssm_g, w_out_a, w_out_s):
    n, d = x2.shape
    tm = MIX_ROWS
    row = lambda i: (i, 0)
    fix = lambda i: (0, 0)
    full = lambda a: pl.BlockSpec(a.shape, fix)
    return pl.pallas_call(
        _mix_out_kernel,
        grid=(n // tm,),
        in_specs=[pl.BlockSpec((tm, d), row), pl.BlockSpec((tm, ATT_WIDTH), row),
                  pl.BlockSpec((tm, SSM_WIDTH), row), full(w_glu_bf), full(b_glu), full(att_g),
                  full(ssm_g), full(w_out_a), full(w_out_s)],
        out_specs=pl.BlockSpec((tm, d), row),
        out_shape=jax.ShapeDtypeStruct((n, d), F32),
        compiler_params=_cparams(("parallel",)),
        name="mix_out",
    )(x2, att, y, w_glu_bf, b_glu, att_g, ssm_g, w_out_a, w_out_s)


def _mem_kv_kernel(mem_ref, g_ref, wk_ref, wv_ref, kg_ref, k_ref, v_ref):
    mn = _rms_rows(mem_ref[...], g_ref[...]).astype(BF16)
    k = _dot(mn, wk_ref[...])
    hd = k.shape[1] // MEM_HEADS
    for h in range(MEM_HEADS):
        k_ref[:, h * hd:(h + 1) * hd] = _rms_rows(k[:, h * hd:(h + 1) * hd], kg_ref[...]).astype(BF16)
    v_ref[...] = _dot(mn, wv_ref[...]).astype(BF16)


def _mem_kv(mem, g, wk_bf, wv_bf, kg):
    bsz, m, d = mem.shape
    fix = lambda b: (0, 0)
    per_b = pl.BlockSpec((None, m, d), lambda b: (b, 0, 0))
    return pl.pallas_call(
        _mem_kv_kernel,
        grid=(bsz,),
        in_specs=[per_b, pl.BlockSpec(g.shape, fix), pl.BlockSpec(wk_bf.shape, fix),
                  pl.BlockSpec(wv_bf.shape, fix), pl.BlockSpec(kg.shape, fix)],
        out_specs=[per_b, per_b],
        out_shape=[jax.ShapeDtypeStruct((bsz, m, d), BF16)] * 2,
        compiler_params=_cparams(("parallel",)),
        name="mem_kv",
    )(mem, g, wk_bf, wv_bf, kg)


def _mem_attn_kernel(h_ref, g_ref, wq_ref, qg_ref, k_ref, v_ref, wo_ref, o_ref):
    h1 = h_ref[...]
    hn = _rms_rows(h1, g_ref[...]).astype(BF16)
    q = _dot(hn, wq_ref[...])
    hd = q.shape[1] // MEM_HEADS
    outs = []
    for h in range(MEM_HEADS):
        sl = slice(h * hd, (h + 1) * hd)
        qh = (_rms_rows(q[:, sl], qg_ref[...]) * (hd ** -0.5)).astype(BF16)
        s = _dot_nt(qh, k_ref[:, sl])
        m = jnp.max(s, axis=-1, keepdims=True)
        p = jnp.exp(s - m)
        l = jnp.sum(p, axis=-1, keepdims=True)
        outs.append(_dot(p.astype(BF16), v_ref[:, sl]) / l)
    o = jnp.concatenate(outs, axis=-1).astype(BF16)
    o_ref[...] = (h1 + _dot(o, wo_ref[...])).T


def _mem_attn(h1, g, wq_bf, qg, kn, v, wo_bf, bsz, seq):
    d = h1.shape[-1]
    m = kn.shape[1]
    tm = MEM_ROWS
    steps = seq // tm
    fix = lambda b, i: (0, 0)
    tok = pl.BlockSpec((None, tm, d), lambda b, i: (b, i, 0))
    per_b = pl.BlockSpec((None, m, d), lambda b, i: (b, 0, 0))
    full = lambda a: pl.BlockSpec(a.shape, fix)
    return pl.pallas_call(
        _mem_attn_kernel,
        grid=(bsz, steps),
        in_specs=[tok, full(g), full(wq_bf), full(qg), per_b, per_b, full(wo_bf)],
        out_specs=pl.BlockSpec((d, tm), lambda b, i: (0, b * steps + i)),
        out_shape=jax.ShapeDtypeStruct((d, bsz * seq), F32),
        compiler_params=_cparams(("parallel", "parallel")),
        name="mem_attn",
    )(h1, g, wq_bf, qg, kn, v, wo_bf)


_CAND_ROWS = 80


def _top16_rows(scores, exact):
    rows = scores.shape[0]
    ridx = lax.broadcasted_iota(jnp.int32, scores.shape, 0).astype(F32) if exact else None
    work = scores
    rank = jnp.full(scores.shape, float(PEER_TOPK), F32)
    vals = []
    for r in range(PEER_TOPK):
        m = jnp.max(work, axis=0, keepdims=True)
        hit = work == m
        if exact:
            first = jnp.min(jnp.where(hit, ridx, float(rows)), axis=0, keepdims=True)
            hit = ridx == first
        rank = jnp.where(hit, float(r), rank)
        work = jnp.where(hit, -jnp.inf, work)
        vals.append(m)
    ranked = jnp.sum(jnp.where(rank < float(PEER_TOPK), 1.0, 0.0), axis=0, keepdims=True)
    return jnp.concatenate(vals, axis=0), rank, ranked


def _peer_head(keys_ref, qt_ref, h, exact):
    nk = PEER_KEYS
    halves = []
    for s in range(2):
        r0 = (2 * h + s) * nk
        st = _dot(keys_ref[2 * h + s], qt_ref[r0:r0 + nk, :].astype(BF16))
        halves.append((st,) + _top16_rows(st, exact))
    (s0, sv0, rank0, ranked0), (s1, sv1, rank1, ranked1) = halves
    pieces = [sv0[0:1] + sv1]
    pieces += [sv0[a:a + 1] + sv1[0:8] for a in range(1, 8)]
    pieces += [sv0[8:16] + sv1[0:1]]
    cand = jnp.concatenate(pieces, axis=0)
    ridx = lax.broadcasted_iota(jnp.int32, cand.shape, 0).astype(F32) if exact else None
    work = cand
    for _ in range(PEER_TOPK):
        m = jnp.max(work, axis=0, keepdims=True)
        hit = work == m
        if exact:
            first = jnp.min(jnp.where(hit, ridx, float(_CAND_ROWS)), axis=0, keepdims=True)
            hit = ridx == first
        work = jnp.where(hit, -jnp.inf, work)
    sel = jnp.where(work == -jnp.inf, 1.0, 0.0)
    chosen = jnp.sum(sel, axis=0, keepdims=True)
    z = jnp.sum(sel * jnp.exp(cand - cand[0:1]), axis=0, keepdims=True)
    counts = [jnp.sum(sel[0:16], axis=0, keepdims=True)]
    counts += [jnp.sum(sel[16 + 8 * (a - 1):24 + 8 * (a - 1)], axis=0, keepdims=True)
               for a in range(1, 8)]
    counts += [sel[72 + a:73 + a] for a in range(8)]
    n_dense = jnp.zeros(rank0.shape, F32)
    for a in range(PEER_TOPK):
        n_dense = jnp.where(rank0 == float(a), counts[a], n_dense)
    a_dense = jnp.exp(s0 - sv0[0:1]) / z
    e_dense = jnp.exp(s1 - sv1[0:1])
    k = float(PEER_TOPK)
    clean = (ranked0 == k) & (ranked1 == k) & (chosen == k)
    return a_dense, n_dense, rank1, e_dense, clean


def _peer_select_kernel(ht_ref, g_ref, wq_ref, keys_ref, hn_ref, a0_ref, n0_ref, b1_ref, e1_ref,
                        qt_ref):
    ht = ht_ref[...]
    ms = jnp.mean(ht * ht, axis=0, keepdims=True)
    hn = (ht * lax.rsqrt(ms + EPS) * g_ref[...]).astype(BF16)
    hn_ref[...] = hn
    qt_ref[...] = _dot(wq_ref[...], hn)

    def write(h, exact):
        a_dense, n_dense, rank1, e_dense, clean = _peer_head(keys_ref, qt_ref, h, exact)
        a0_ref[h] = a_dense
        n0_ref[h] = n_dense
        b1_ref[h] = rank1.astype(BF16)
        e1_ref[h] = e_dense.astype(BF16)
        return clean

    tied = jnp.zeros((1, ht.shape[1]), F32)
    for h in range(PEER_HEADS):
        tied = jnp.maximum(tied, jnp.where(write(h, exact=False), 0.0, 1.0))

    @pl.when(jnp.max(tied) > 0.0)
    def _():
        for h in range(PEER_HEADS):
            write(h, exact=True)


def _peer_select(h2t, g_col, wq_t_bf, keys_bf):
    d, n = h2t.shape
    tt = PEER_SEL_TOKENS
    col = lambda i: (0, i)
    dense = pl.BlockSpec((PEER_HEADS, PEER_KEYS, tt), lambda i: (0, 0, i))
    dense_shape = jax.ShapeDtypeStruct((PEER_HEADS, PEER_KEYS, n), BF16)
    per_key = dense
    per_key_shape = jax.ShapeDtypeStruct((PEER_HEADS, PEER_KEYS, n), F32)
    return pl.pallas_call(
        _peer_select_kernel,
        grid=(n // tt,),
        in_specs=[pl.BlockSpec((d, tt), col), pl.BlockSpec(g_col.shape, lambda i: (0, 0)),
                  pl.BlockSpec(wq_t_bf.shape, lambda i: (0, 0)),
                  pl.BlockSpec(keys_bf.shape, lambda i: (0, 0, 0))],
        out_specs=[pl.BlockSpec((d, tt), col), per_key, per_key, dense, dense],
        out_shape=[jax.ShapeDtypeStruct((d, n), BF16), per_key_shape, per_key_shape, dense_shape,
                   dense_shape],
        scratch_shapes=[pltpu.VMEM((wq_t_bf.shape[0], tt), F32)],
        compiler_params=_cparams(("parallel",)),
        name="peer_select",
    )(h2t, g_col, wq_t_bf, keys_bf)


def _peer_dense_kernel(ht_ref, hn_ref, u_ref, vt_ref, a0_ref, n0_ref, b1_ref, e1_ref, o_ref,
                       acc_ref, act0_ref, act1_ref, y0_ref, y1_ref):
    j = pl.program_id(1)

    @pl.when(j == 0)
    def _():
        acc_ref[...] = jnp.zeros_like(acc_ref)

    nk = PEER_KEYS
    kc = PEER_DENSE_CHUNK
    n_chunks = PEER_EXPERT_TILE // kc
    act_refs = (act0_ref, act1_ref)
    y_refs = (y0_ref, y1_ref)

    def activations(c):
        act_refs[c % 2][...] = _dot(u_ref[c * kc:(c + 1) * kc, :], hn_ref[...])

    def gate(c):
        act_ref, y_ref = act_refs[c % 2], y_refs[c % 2]
        gr = PEER_GATE_ROWS
        tokens = act_ref.shape[1]
        for ii in range(kc // nk):
            i = c * (kc // nk) + ii
            w = [None] * (nk // gr)
            for h in range(PEER_HEADS):
                n_i = jnp.broadcast_to(n0_ref[h, i:i + 1, :], (gr, tokens)).astype(BF16)
                a_i = jnp.broadcast_to(a0_ref[h, i:i + 1, :], (gr, tokens)).astype(BF16)
                for jb in range(nk // gr):
                    js = slice(jb * gr, (jb + 1) * gr)
                    term = a_i * jnp.where(b1_ref[h, js, :] < n_i, e1_ref[h, js, :], 0.0)
                    w[jb] = term if w[jb] is None else w[jb] + term
            for jb in range(nk // gr):
                rows = slice(ii * nk + jb * gr, ii * nk + (jb + 1) * gr)
                y_ref[rows, :] = w[jb] * _gelu(act_ref[rows, :]).astype(BF16)

    def project(c):
        acc_ref[...] += _dot(vt_ref[:, c * kc:(c + 1) * kc], y_refs[c % 2][...])

    activations(0)
    for c in range(n_chunks):
        if c + 1 < n_chunks:
            activations(c + 1)
        gate(c)
        if c >= 1:
            project(c - 1)
    project(n_chunks - 1)

    @pl.when(j == pl.num_programs(1) - 1)
    def _():
        o_ref[...] = (ht_ref[...] + acc_ref[...]).T


def _peer_dense(h2t, hn_t, u_bf, vt_bf, a0, n0, b1, e1):
    d, n = h2t.shape
    e = u_bf.shape[0]
    tt, te = PEER_TOKENS, PEER_EXPERT_TILE
    ipt = te // PEER_KEYS
    col = lambda i, j: (0, i)
    per_i = pl.BlockSpec((PEER_HEADS, ipt, tt), lambda i, j: (0, j, i))
    per_t = pl.BlockSpec((PEER_HEADS, PEER_KEYS, tt), lambda i, j: (0, 0, i))
    return pl.pallas_call(
        _peer_dense_kernel,
        grid=(n // tt, e // te),
        in_specs=[pl.BlockSpec((d, tt), col), pl.BlockSpec((d, tt), col),
                  pl.BlockSpec((te, d), lambda i, j: (j, 0)),
                  pl.BlockSpec((d, te), lambda i, j: (0, j)),
                  per_i, per_i, per_t, per_t],
        out_specs=pl.BlockSpec((tt, d), lambda i, j: (i, 0)),
        out_shape=jax.ShapeDtypeStruct((n, d), F32),
        scratch_shapes=[pltpu.VMEM((d, tt), F32)]
        + [pltpu.VMEM((PEER_DENSE_CHUNK, tt), F32)] * 2 + [pltpu.VMEM((PEER_DENSE_CHUNK, tt), BF16)] * 2,
        compiler_params=_cparams(("parallel", "arbitrary")),
        name="peer_dense",
    )(h2t, hn_t, u_bf, vt_bf, a0, n0, b1, e1)


def _layer(h, mem, p):
    bsz, seq, d = h.shape
    n = bsz * seq
    x2 = h.reshape(n, d)
    row = lambda a: a.reshape(1, -1)

    seg = jnp.asarray(np.kron(np.eye(ATT_HEADS), np.ones((ATT_HEAD_DIM, ATT_HEAD_DIM))), BF16)
    q, k, v, u = _inproj(
        x2, row(p["norm_mix_g"]), p["w_in"].astype(BF16), seg,
        row(jnp.tile(p["att_q_g"], ATT_HEADS)),
        row(jnp.tile(p["att_k_g"], ATT_HEADS)))
    shp = (bsz, seq, ATT_WIDTH)
    att = _attention(q.reshape(shp), k.reshape(shp), v.reshape(shp),
                     _band_bias_tile(p["rel_bias"]), bsz, seq).reshape(n, ATT_WIDTH)

    t, g, c = SSM_T, SSM_GROUPS, SSM_GROUP
    chunks = seq // t
    n_steps = max(1, int(math.ceil(math.log2(chunks))))
    k_lag, bcat, ccat, apr, api = _ssm_params(
        p["ssm_lam_re"], p["ssm_lam_im"], p["ssm_log_step"], p["ssm_b_re"], p["ssm_b_im"],
        p["ssm_c_re"], p["ssm_c_im"], n_steps)
    u_g = u.reshape(bsz, chunks, t, g, c).transpose(3, 0, 1, 2, 4).reshape(g, bsz * chunks, t * c)
    d_flat = jnp.tile(p["ssm_d"], (1, t))[:, None, :]
    y_g = _ssm_main(u_g, k_lag, bcat, ccat, apr, api, d_flat, chunks, n_steps)
    y = y_g.reshape(g, bsz, chunks, t, c).transpose(1, 2, 3, 0, 4).reshape(n, SSM_WIDTH)

    w_out = p["w_out"].astype(BF16)
    h1 = _mix_out(x2, att, y, p["ssm_w_glu"].astype(BF16), row(p["ssm_b_glu"]),
                  row(p["att_out_g"]), row(p["ssm_out_g"]), w_out[:ATT_WIDTH], w_out[ATT_WIDTH:])

    kn, vm = _mem_kv(mem, row(p["norm_memkv_g"]), p["w_mem_k"].astype(BF16),
                     p["w_mem_v"].astype(BF16), row(p["mem_k_g"]))
    h2t = _mem_attn(h1.reshape(bsz, seq, d), row(p["norm_mem_g"]), p["w_mem_q"].astype(BF16),
                    row(p["mem_q_g"]), kn, vm, p["w_mem_o"].astype(BF16), bsz, seq)

    keys_bf = p["peer_keys"].reshape(PEER_HEADS * 2, PEER_KEYS, -1).astype(BF16)
    hn_t, a0, n0, b1, e1 = _peer_select(h2t, p["norm_peer_g"].reshape(d, 1),
                                        p["w_peer_q"].T.astype(BF16), keys_bf)
    h3 = _peer_dense(h2t, hn_t, p["peer_u"].astype(BF16), p["peer_v"].T.astype(BF16),
                     a0, n0, b1, e1)
    return h3.reshape(bsz, seq, d)


_PARAM_NAMES = (
    "norm_mix_g", "w_in", "att_q_g", "att_k_g", "rel_bias", "ssm_lam_re", "ssm_lam_im",
    "ssm_log_step", "ssm_b_re", "ssm_b_im", "ssm_c_re", "ssm_c_im", "ssm_d", "ssm_w_glu",
    "ssm_b_glu", "att_out_g", "ssm_out_g", "w_out", "norm_mem_g", "norm_memkv_g", "w_mem_q",
    "w_mem_k", "w_mem_v", "mem_q_g", "mem_k_g", "w_mem_o", "norm_peer_g", "w_peer_q",
    "peer_keys", "peer_u", "peer_v")


def kernel(x, mem, norm_mix_g, w_in, att_q_g, att_k_g, rel_bias, ssm_lam_re, ssm_lam_im, ssm_log_step, ssm_b_re, ssm_b_im, ssm_c_re, ssm_c_im, ssm_d, ssm_w_glu, ssm_b_glu, att_out_g, ssm_out_g, w_out, norm_mem_g, norm_memkv_g, w_mem_q, w_mem_k, w_mem_v, mem_q_g, mem_k_g, w_mem_o, norm_peer_g, w_peer_q, peer_keys, peer_u, peer_v):
    stacked = (norm_mix_g, w_in, att_q_g, att_k_g, rel_bias, ssm_lam_re, ssm_lam_im, ssm_log_step,
               ssm_b_re, ssm_b_im, ssm_c_re, ssm_c_im, ssm_d, ssm_w_glu, ssm_b_glu, att_out_g,
               ssm_out_g, w_out, norm_mem_g, norm_memkv_g, w_mem_q, w_mem_k, w_mem_v, mem_q_g,
               mem_k_g, w_mem_o, norm_peer_g, w_peer_q, peer_keys, peer_u, peer_v)
    h = x
    for l in range(norm_mix_g.shape[0]):
        h = _layer(h, mem, {name: a[l] for name, a in zip(_PARAM_NAMES, stacked)})
    return h
```

```python
import functools
import math

import numpy as np
import jax
import jax.numpy as jnp
from jax import lax
from jax.experimental import pallas as pl
from jax.experimental.pallas import tpu as pltpu

F32 = jnp.float32
BF16 = jnp.bfloat16

EPS = 1e-6
NEG_INF = -1e30

CHUNK = 64
ATT_HEADS = 8
ATT_HEAD_DIM = 64
ATT_WIDTH = ATT_HEADS * ATT_HEAD_DIM
LEFT_CHUNKS = 8
REL_MAX = 128
SSM_GROUP = 16
SSM_GROUPS = 32
SSM_STATE = 64
SSM_WIDTH = SSM_GROUP * SSM_GROUPS
MEM_HEADS = 4
PEER_HEADS = 8
PEER_KEYS = 128
PEER_TOPK = 16

INPROJ_ROWS = 512
ATT_QBLOCK = 512
ATT_GROUP = 256
ATT_WINDOW = ATT_GROUP + LEFT_CHUNKS * CHUNK
SSM_T = 32
MIX_ROWS = 512
MEM_ROWS = 512
PEER_SEL_TOKENS = 256
PEER_TOKENS = 512
PEER_EXPERT_TILE = 2048
PEER_DENSE_CHUNK = 256
PEER_GATE_ROWS = 16

VMEM_LIMIT = 48 * 1024 * 1024


def _cparams(sem):
    return pltpu.CompilerParams(dimension_semantics=sem, vmem_limit_bytes=VMEM_LIMIT)


def _rms_rows(x, gain):
    ms = jnp.mean(x * x, axis=-1, keepdims=True)
    return x * lax.rsqrt(ms + EPS) * gain


def _dot(a, b):
    return jnp.dot(a, b, preferred_element_type=F32)


def _dot_nt(a, b):
    return lax.dot_general(a, b, (((1,), (1,)), ((), ())), preferred_element_type=F32)


def _inproj_kernel(x_ref, g_ref, w_ref, seg_ref, qg_ref, kg_ref, q_ref, k_ref, v_ref, u_ref):
    xn = _rms_rows(x_ref[...], g_ref[...])
    proj = _dot(xn.astype(BF16), w_ref[...])
    seg = seg_ref[...]

    def head_norm(z, gain):
        zz = z * z
        hi = zz.astype(BF16)
        lo = (zz - hi.astype(F32)).astype(BF16)
        ss = _dot(hi, seg) + _dot(lo, seg)
        return z * lax.rsqrt(ss * (1.0 / ATT_HEAD_DIM) + EPS) * gain

    q = head_norm(proj[:, :ATT_WIDTH], qg_ref[...]) * (ATT_HEAD_DIM ** -0.5)
    q_ref[...] = q.astype(BF16)
    k_ref[...] = head_norm(proj[:, ATT_WIDTH:2 * ATT_WIDTH], kg_ref[...]).astype(BF16)
    v_ref[...] = proj[:, 2 * ATT_WIDTH:3 * ATT_WIDTH].astype(BF16)
    u_ref[...] = proj[:, 3 * ATT_WIDTH:]


def _inproj(x2, g, w_in_bf, seg, qg, kg):
    n, d = x2.shape
    tm = INPROJ_ROWS
    row = lambda i: (i, 0)
    fix = lambda i: (0, 0)
    outw = ATT_WIDTH
    return pl.pallas_call(
        _inproj_kernel,
        grid=(n // tm,),
        in_specs=[pl.BlockSpec((tm, d), row), pl.BlockSpec((1, d), fix),
                  pl.BlockSpec(w_in_bf.shape, fix), pl.BlockSpec(seg.shape, fix),
                  pl.BlockSpec((1, outw), fix), pl.BlockSpec((1, outw), fix)],
        out_specs=[pl.BlockSpec((tm, outw), row)] * 3 + [pl.BlockSpec((tm, SSM_WIDTH), row)],
        out_shape=[jax.ShapeDtypeStruct((n, outw), BF16)] * 3
        + [jax.ShapeDtypeStruct((n, SSM_WIDTH), F32)],
        compiler_params=_cparams(("parallel",)),
        name="inproj",
    )(x2, g, w_in_bf, seg, qg, kg)


def _attn_kernel(q_ref, kp_ref, kc_ref, vp_ref, vc_ref, bias_ref, o_ref):
    blk = pl.program_id(1)
    kk = jnp.concatenate([kp_ref[...], kc_ref[...]], axis=0)
    vv = jnp.concatenate([vp_ref[...], vc_ref[...]], axis=0)
    chunks_per_block = ATT_QBLOCK // CHUNK
    key_chunk = lax.broadcasted_iota(jnp.int32, (1, ATT_WINDOW), 1) // CHUNK
    for gi in range(ATT_QBLOCK // ATT_GROUP):
        r0 = gi * ATT_GROUP
        q = q_ref[r0:r0 + ATT_GROUP, :]
        kwin = kk[r0:r0 + ATT_WINDOW, :]
        vwin = vv[r0:r0 + ATT_WINDOW, :]
        abs_chunk = blk * chunks_per_block + gi * (ATT_GROUP // CHUNK) + key_chunk - LEFT_CHUNKS
        valid = abs_chunk >= 0
        for h in range(ATT_HEADS):
            c0 = h * ATT_HEAD_DIM
            s = _dot_nt(q[:, c0:c0 + ATT_HEAD_DIM], kwin[:, c0:c0 + ATT_HEAD_DIM])
            s = jnp.where(valid, s + bias_ref[h], NEG_INF)
            m = jnp.max(s, axis=-1, keepdims=True)
            p = jnp.exp(s - m)
            l = jnp.sum(p, axis=-1, keepdims=True)
            o = _dot(p.astype(BF16), vwin[:, c0:c0 + ATT_HEAD_DIM])
            o_ref[r0:r0 + ATT_GROUP, c0:c0 + ATT_HEAD_DIM] = o / l


def _attention(q, k, v, bias, bsz, seq):
    qb = ATT_QBLOCK
    cur = lambda b, i: (b, i, 0)
    prev = lambda b, i: (b, jnp.maximum(i - 1, 0), 0)
    blk = (None, qb, ATT_WIDTH)
    return pl.pallas_call(
        _attn_kernel,
        grid=(bsz, seq // qb),
        in_specs=[pl.BlockSpec(blk, cur), pl.BlockSpec(blk, prev), pl.BlockSpec(blk, cur),
                  pl.BlockSpec(blk, prev), pl.BlockSpec(blk, cur),
                  pl.BlockSpec(bias.shape, lambda b, i: (0, 0, 0))],
        out_specs=pl.BlockSpec(blk, cur),
        out_shape=jax.ShapeDtypeStruct((bsz, seq, ATT_WIDTH), F32),
        compiler_params=_cparams(("parallel", "parallel")),
        name="attn",
    )(q, k, k, v, v, bias)


def _band_bias_tile(rel_bias):
    heads = rel_bias.shape[0]
    period = 1024
    delta = (np.arange(period) + (ATT_GROUP - 1)) % period - (ATT_GROUP - 1)
    bucket = np.clip(LEFT_CHUNKS * CHUNK - delta, -(CHUNK - 1), REL_MAX) + (CHUNK - 1)
    vec = rel_bias[:, bucket]
    flat = jnp.tile(vec, (1, ATT_GROUP))[:, :ATT_GROUP * (period - 1)]
    tile = flat.reshape(heads, ATT_GROUP, period - 1)[:, :, :ATT_WINDOW]
    r = np.arange(ATT_GROUP)[:, None]
    s = np.arange(ATT_WINDOW)[None, :]
    qc, kc = r // CHUNK, s // CHUNK
    band = (kc >= qc) & (kc <= qc + LEFT_CHUNKS)
    return jnp.where(band[None], tile, NEG_INF)


def _ssm_param_kernel(lr_r_ref, li_r_ref, lr_c_ref, li_c_ref, ls_ref, taus_ref,
                      brt_ref, bit_ref, crt_ref, cit_ref,
                      k_ref, bcat_ref, ccat_ref, apr_ref, api_ref):
    t = SSM_T
    p = SSM_STATE
    step = jnp.exp(ls_ref[...])
    lr_r, li_r = lr_r_ref[...], li_r_ref[...]
    lr_c, li_c = lr_c_ref[...], li_c_ref[...]

    def zoh(lr, li):
        mag = jnp.exp(lr * step)
        ar = mag * jnp.cos(li * step)
        ai = mag * jnp.sin(li * step)
        zr = ar - 1.0
        den = lr * lr + li * li
        return (zr * lr + ai * li) / den, (ai * lr - zr * li) / den

    def power(lr, li, tau):
        mag = jnp.exp(lr * step * tau)
        ph = li * step * tau
        return mag * jnp.cos(ph), mag * jnp.sin(ph)

    fr_r, fi_r = zoh(lr_r, li_r)
    brt, bit = brt_ref[...], bit_ref[...]
    bbr_t = fr_r * brt - fi_r * bit
    bbi_t = fr_r * bit + fi_r * brt
    crt, cit = crt_ref[...], cit_ref[...]
    lane_t = (lax.broadcasted_iota(jnp.int32, (1, t * SSM_GROUP), 1) // SSM_GROUP).astype(F32)

    p0_r, p0_i = power(lr_c, li_c, lane_t)
    hp = lax.Precision.HIGHEST
    k_ref[...] = (jnp.dot(bbr_t, crt * p0_r - cit * p0_i, precision=hp, preferred_element_type=F32)
                  - jnp.dot(bbi_t, crt * p0_i + cit * p0_r, precision=hp, preferred_element_type=F32))

    tau = lax.broadcasted_iota(jnp.int32, (t, 1), 0).astype(F32)
    rv_r, rv_i = power(lr_r, li_r, float(t - 1) - tau)
    b_re = rv_r[:, None, :] * bbr_t[None] - rv_i[:, None, :] * bbi_t[None]
    b_im = rv_r[:, None, :] * bbi_t[None] + rv_i[:, None, :] * bbr_t[None]
    bcat_ref[:, 0:p] = b_re.reshape(t * SSM_GROUP, p)
    bcat_ref[:, p:2 * p] = b_im.reshape(t * SSM_GROUP, p)

    p1_r, p1_i = power(lr_c, li_c, lane_t + 1.0)
    ccat_ref[0:p, :] = crt * p1_r - cit * p1_i
    ccat_ref[p:2 * p, :] = -(crt * p1_i + cit * p1_r)

    dr, di = power(lr_r, li_r, taus_ref[...])
    apr_ref[:, 0:p] = dr
    apr_ref[:, p:2 * p] = dr
    api_ref[:, 0:p] = -di
    api_ref[:, p:2 * p] = di


def _ssm_params(lam_re, lam_im, log_step, b_re, b_im, c_re, c_im, n_steps):
    g, p, c = SSM_GROUPS, SSM_STATE, SSM_GROUP
    t = SSM_T
    taus = jnp.asarray([[float(t * 2 ** k)] for k in range(n_steps)], F32)
    c_re_t = jnp.swapaxes(c_re, 1, 2)
    c_im_t = jnp.swapaxes(c_im, 1, 2)
    args = [
        lam_re[:, None, :], lam_im[:, None, :], lam_re[:, :, None], lam_im[:, :, None],
        log_step[:, None, None], taus,
        jnp.swapaxes(b_re, 1, 2), jnp.swapaxes(b_im, 1, 2),
        jnp.tile(c_re_t, (1, 1, t)), jnp.tile(c_im_t, (1, 1, t)),
    ]

    def spec(a):
        if a.ndim == 2:
            return pl.BlockSpec(a.shape, lambda i: (0, 0))
        return pl.BlockSpec((None,) + a.shape[1:], lambda i: (i, 0, 0))

    out_dims = [(c, t * c), (t * c, 2 * p), (2 * p, t * c), (n_steps, 2 * p), (n_steps, 2 * p)]
    return pl.pallas_call(
        _ssm_param_kernel,
        grid=(g,),
        in_specs=[spec(a) for a in args],
        out_specs=[pl.BlockSpec((None,) + d, lambda i: (i, 0, 0)) for d in out_dims],
        out_shape=[jax.ShapeDtypeStruct((g,) + d, F32) for d in out_dims],
        compiler_params=_cparams(("parallel",)),
        name="ssm_params",
    )(*args)


def _ssm_main_kernel(u_ref, k_ref, bcat_ref, ccat_ref, apr_ref, api_ref, d_ref, y_ref, m_ref,
                     *, rows_per_batch, n_steps):
    k = k_ref[...]
    lane = lax.broadcasted_iota(jnp.int32, k.shape, 1)
    for s in range(SSM_T):
        sh = s * SSM_GROUP
        blk = k if s == 0 else jnp.where(lane >= sh, pltpu.roll(k, sh, 1), 0.0)
        m_ref[s * SSM_GROUP:(s + 1) * SSM_GROUP, :] = blk.astype(BF16)
    u = u_ref[...]
    ub = u.astype(BF16)
    y = _dot(ub, m_ref[...])
    x = _dot(ub, bcat_ref[...].astype(BF16))
    r = x.shape[0]
    k_in_batch = lax.broadcasted_iota(jnp.int32, (r, 1), 0) % rows_per_batch
    half = SSM_STATE
    for k in range(n_steps):
        d = 2 ** k
        sh = jnp.where(k_in_batch >= d, pltpu.roll(x, d, 0), 0.0)
        x = x + sh * apr_ref[k:k + 1, :] + pltpu.roll(sh, half, 1) * api_ref[k:k + 1, :]
    x_prev = jnp.where(k_in_batch >= 1, pltpu.roll(x, 1, 0), 0.0)
    y = y + _dot(x_prev.astype(BF16), ccat_ref[...].astype(BF16))
    y_ref[...] = y + u * d_ref[...]


def _ssm_main(u_g, k_lag, bcat, ccat, apr, api, d_flat, rows_per_batch, n_steps):
    g, r, w = u_g.shape
    per_g = lambda a: pl.BlockSpec((None,) + a.shape[1:], lambda i: (i, 0, 0))
    kern = functools.partial(_ssm_main_kernel, rows_per_batch=rows_per_batch, n_steps=n_steps)
    return pl.pallas_call(
        kern,
        grid=(g,),
        in_specs=[per_g(a) for a in (u_g, k_lag, bcat, ccat, apr, api, d_flat)],
        out_specs=per_g(u_g),
        out_shape=jax.ShapeDtypeStruct((g, r, w), F32),
        scratch_shapes=[pltpu.VMEM((w, w), BF16)],
        compiler_params=_cparams(("parallel",)),
        name="ssm_main",
    )(u_g, k_lag, bcat, ccat, apr, api, d_flat)


def _gelu(x):
    return 0.5 * x * (1.0 + lax.erf(x * (1.0 / math.sqrt(2.0))))


def _mix_out_kernel(x_ref, att_ref, y_ref, wg_ref, bg_ref, ag_ref, sg_ref, wa_ref, ws_ref, h_ref):
    y = _gelu(y_ref[...])
    z = _dot(y.astype(BF16), wg_ref[...]) + bg_ref[...]
    ssm = y * jax.nn.sigmoid(z)
    a = _rms_rows(att_ref[...], ag_ref[...])
    s = _rms_rows(ssm, sg_ref[...])
    h_ref[...] = x_ref[...] + _dot(a.astype(BF16), wa_ref[...]) + _dot(s.astype(BF16), ws_ref[...])


def _mix_out(x2, att, y, w_glu_bf, b_glu, att_g, ssm_g, w_out_a, w_out_s):
    n, d = x2.shape
    tm = MIX_ROWS
    row = lambda i: (i, 0)
    fix = lambda i: (0, 0)
    full = lambda a: pl.BlockSpec(a.shape, fix)
    return pl.pallas_call(
        _mix_out_kernel,
        grid=(n // tm,),
        in_specs=[pl.BlockSpec((tm, d), row), pl.BlockSpec((tm, ATT_WIDTH), row),
                  pl.BlockSpec((tm, SSM_WIDTH), row), full(w_glu_bf), full(b_glu), full(att_g),
                  full(ssm_g), full(w_out_a), full(w_out_s)],
        out_specs=pl.BlockSpec((tm, d), row),
        out_shape=jax.ShapeDtypeStruct((n, d), F32),
        compiler_params=_cparams(("parallel",)),
        name="mix_out",
    )(x2, att, y, w_glu_bf, b_glu, att_g, ssm_g, w_out_a, w_out_s)


def _mem_kv_kernel(mem_ref, g_ref, wk_ref, wv_ref, kg_ref, k_ref, v_ref):
    mn = _rms_rows(mem_ref[...], g_ref[...]).astype(BF16)
    k = _dot(mn, wk_ref[...])
    hd = k.shape[1] // MEM_HEADS
    for h in range(MEM_HEADS):
        k_ref[:, h * hd:(h + 1) * hd] = _rms_rows(k[:, h * hd:(h + 1) * hd], kg_ref[...]).astype(BF16)
    v_ref[...] = _dot(mn, wv_ref[...]).astype(BF16)


def _mem_kv(mem, g, wk_bf, wv_bf, kg):
    bsz, m, d = mem.shape
    fix = lambda b: (0, 0)
    per_b = pl.BlockSpec((None, m, d), lambda b: (b, 0, 0))
    return pl.pallas_call(
        _mem_kv_kernel,
        grid=(bsz,),
        in_specs=[per_b, pl.BlockSpec(g.shape, fix), pl.BlockSpec(wk_bf.shape, fix),
                  pl.BlockSpec(wv_bf.shape, fix), pl.BlockSpec(kg.shape, fix)],
        out_specs=[per_b, per_b],
        out_shape=[jax.ShapeDtypeStruct((bsz, m, d), BF16)] * 2,
        compiler_params=_cparams(("parallel",)),
        name="mem_kv",
    )(mem, g, wk_bf, wv_bf, kg)


def _mem_attn_kernel(h_ref, g_ref, wq_ref, qg_ref, k_ref, v_ref, wo_ref, o_ref):
    h1 = h_ref[...]
    hn = _rms_rows(h1, g_ref[...]).astype(BF16)
    q = _dot(hn, wq_ref[...])
    hd = q.shape[1] // MEM_HEADS
    outs = []
    for h in range(MEM_HEADS):
        sl = slice(h * hd, (h + 1) * hd)
        qh = (_rms_rows(q[:, sl], qg_ref[...]) * (hd ** -0.5)).astype(BF16)
        s = _dot_nt(qh, k_ref[:, sl])
        m = jnp.max(s, axis=-1, keepdims=True)
        p = jnp.exp(s - m)
        l = jnp.sum(p, axis=-1, keepdims=True)
        outs.append(_dot(p.astype(BF16), v_ref[:, sl]) / l)
    o = jnp.concatenate(outs, axis=-1).astype(BF16)
    o_ref[...] = (h1 + _dot(o, wo_ref[...])).T


def _mem_attn(h1, g, wq_bf, qg, kn, v, wo_bf, bsz, seq):
    d = h1.shape[-1]
    m = kn.shape[1]
    tm = MEM_ROWS
    steps = seq // tm
    fix = lambda b, i: (0, 0)
    tok = pl.BlockSpec((None, tm, d), lambda b, i: (b, i, 0))
    per_b = pl.BlockSpec((None, m, d), lambda b, i: (b, 0, 0))
    full = lambda a: pl.BlockSpec(a.shape, fix)
    return pl.pallas_call(
        _mem_attn_kernel,
        grid=(bsz, steps),
        in_specs=[tok, full(g), full(wq_bf), full(qg), per_b, per_b, full(wo_bf)],
        out_specs=pl.BlockSpec((d, tm), lambda b, i: (0, b * steps + i)),
        out_shape=jax.ShapeDtypeStruct((d, bsz * seq), F32),
        compiler_params=_cparams(("parallel", "parallel")),
        name="mem_attn",
    )(h1, g, wq_bf, qg, kn, v, wo_bf)


_CAND_ROWS = 80


def _top16_rows(scores, exact):
    rows = scores.shape[0]
    ridx = lax.broadcasted_iota(jnp.int32, scores.shape, 0).astype(F32) if exact else None
    work = scores
    rank = jnp.full(scores.shape, float(PEER_TOPK), F32)
    vals = []
    for r in range(PEER_TOPK):
        m = jnp.max(work, axis=0, keepdims=True)
        hit = work == m
        if exact:
            first = jnp.min(jnp.where(hit, ridx, float(rows)), axis=0, keepdims=True)
            hit = ridx == first
        rank = jnp.where(hit, float(r), rank)
        work = jnp.where(hit, -jnp.inf, work)
        vals.append(m)
    ranked = jnp.sum(jnp.where(rank < float(PEER_TOPK), 1.0, 0.0), axis=0, keepdims=True)
    return jnp.concatenate(vals, axis=0), rank, ranked


def _peer_head(keys_ref, qt_ref, h, exact):
    nk = PEER_KEYS
    halves = []
    for s in range(2):
        r0 = (2 * h + s) * nk
        st = _dot(keys_ref[2 * h + s], qt_ref[r0:r0 + nk, :].astype(BF16))
        halves.append((st,) + _top16_rows(st, exact))
    (s0, sv0, rank0, ranked0), (s1, sv1, rank1, ranked1) = halves
    pieces = [sv0[0:1] + sv1]
    pieces += [sv0[a:a + 1] + sv1[0:8] for a in range(1, 8)]
    pieces += [sv0[8:16] + sv1[0:1]]
    cand = jnp.concatenate(pieces, axis=0)
    ridx = lax.broadcasted_iota(jnp.int32, cand.shape, 0).astype(F32) if exact else None
    work = cand
    for _ in range(PEER_TOPK):
        m = jnp.max(work, axis=0, keepdims=True)
        hit = work == m
        if exact:
            first = jnp.min(jnp.where(hit, ridx, float(_CAND_ROWS)), axis=0, keepdims=True)
            hit = ridx == first
        work = jnp.where(hit, -jnp.inf, work)
    sel = jnp.where(work == -jnp.inf, 1.0, 0.0)
    chosen = jnp.sum(sel, axis=0, keepdims=True)
    z = jnp.sum(sel * jnp.exp(cand - cand[0:1]), axis=0, keepdims=True)
    counts = [jnp.sum(sel[0:16], axis=0, keepdims=True)]
    counts += [jnp.sum(sel[16 + 8 * (a - 1):24 + 8 * (a - 1)], axis=0, keepdims=True)
               for a in range(1, 8)]
    counts += [sel[72 + a:73 + a] for a in range(8)]
    n_dense = jnp.zeros(rank0.shape, F32)
    for a in range(PEER_TOPK):
        n_dense = jnp.where(rank0 == float(a), counts[a], n_dense)
    a_dense = jnp.exp(s0 - sv0[0:1]) / z
    e_dense = jnp.exp(s1 - sv1[0:1])
    k = float(PEER_TOPK)
    clean = (ranked0 == k) & (ranked1 == k) & (chosen == k)
    return a_dense, n_dense, rank1, e_dense, clean


def _peer_select_kernel(ht_ref, g_ref, wq_ref, keys_ref, hn_ref, a0_ref, n0_ref, b1_ref, e1_ref,
                        qt_ref):
    ht = ht_ref[...]
    ms = jnp.mean(ht * ht, axis=0, keepdims=True)
    hn = (ht * lax.rsqrt(ms + EPS) * g_ref[...]).astype(BF16)
    hn_ref[...] = hn
    qt_ref[...] = _dot(wq_ref[...], hn)

    def write(h, exact):
        a_dense, n_dense, rank1, e_dense, clean = _peer_head(keys_ref, qt_ref, h, exact)
        a0_ref[h] = a_dense
        n0_ref[h] = n_dense
        b1_ref[h] = rank1.astype(BF16)
        e1_ref[h] = e_dense.astype(BF16)
        return clean

    def redo(h):
        write(h, exact=True)

    tied = [jnp.max(jnp.where(write(h, exact=False), 0.0, 1.0)) for h in range(PEER_HEADS)]
    for h in range(PEER_HEADS):
        pl.when(tied[h] > 0.0)(functools.partial(redo, h))


def _peer_select(h2t, g_col, wq_t_bf, keys_bf):
    d, n = h2t.shape
    tt = PEER_SEL_TOKENS
    col = lambda i: (0, i)
    dense = pl.BlockSpec((PEER_HEADS, PEER_KEYS, tt), lambda i: (0, 0, i))
    dense_shape = jax.ShapeDtypeStruct((PEER_HEADS, PEER_KEYS, n), BF16)
    per_key = dense
    per_key_shape = jax.ShapeDtypeStruct((PEER_HEADS, PEER_KEYS, n), F32)
    return pl.pallas_call(
        _peer_select_kernel,
        grid=(n // tt,),
        in_specs=[pl.BlockSpec((d, tt), col), pl.BlockSpec(g_col.shape, lambda i: (0, 0)),
                  pl.BlockSpec(wq_t_bf.shape, lambda i: (0, 0)),
                  pl.BlockSpec(keys_bf.shape, lambda i: (0, 0, 0))],
        out_specs=[pl.BlockSpec((d, tt), col), per_key, per_key, dense, dense],
        out_shape=[jax.ShapeDtypeStruct((d, n), BF16), per_key_shape, per_key_shape, dense_shape,
                   dense_shape],
        scratch_shapes=[pltpu.VMEM((wq_t_bf.shape[0], tt), F32)],
        compiler_params=_cparams(("parallel",)),
        name="peer_select",
    )(h2t, g_col, wq_t_bf, keys_bf)


def _peer_dense_kernel(ht_ref, hn_ref, u_ref, vt_ref, a0_ref, n0_ref, b1_ref, e1_ref, o_ref,
                       acc_ref, act0_ref, act1_ref, y0_ref, y1_ref):
    j = pl.program_id(1)

    @pl.when(j == 0)
    def _():
        acc_ref[...] = jnp.zeros_like(acc_ref)

    nk = PEER_KEYS
    kc = PEER_DENSE_CHUNK
    n_chunks = PEER_EXPERT_TILE // kc
    act_refs = (act0_ref, act1_ref)
    y_refs = (y0_ref, y1_ref)

    def activations(c):
        act_refs[c % 2][...] = _dot(u_ref[c * kc:(c + 1) * kc, :], hn_ref[...])

    def gate(c):
        act_ref, y_ref = act_refs[c % 2], y_refs[c % 2]
        gr = PEER_GATE_ROWS
        tokens = act_ref.shape[1]
        for ii in range(kc // nk):
            i = c * (kc // nk) + ii
            w = [None] * (nk // gr)
            for h in range(PEER_HEADS):
                n_i = jnp.broadcast_to(n0_ref[h, i:i + 1, :], (gr, tokens)).astype(BF16)
                a_i = jnp.broadcast_to(a0_ref[h, i:i + 1, :], (gr, tokens)).astype(BF16)
                for jb in range(nk // gr):
                    js = slice(jb * gr, (jb + 1) * gr)
                    term = a_i * jnp.where(b1_ref[h, js, :] < n_i, e1_ref[h, js, :], 0.0)
                    w[jb] = term if w[jb] is None else w[jb] + term
            for jb in range(nk // gr):
                rows = slice(ii * nk + jb * gr, ii * nk + (jb + 1) * gr)
                y_ref[rows, :] = w[jb] * _gelu(act_ref[rows, :]).astype(BF16)

    def project(c):
        acc_ref[...] += _dot(vt_ref[:, c * kc:(c + 1) * kc], y_refs[c % 2][...])

    activations(0)
    for c in range(n_chunks):
        if c + 1 < n_chunks:
            activations(c + 1)
        gate(c)
        if c >= 1:
            project(c - 1)
    project(n_chunks - 1)

    @pl.when(j == pl.num_programs(1) - 1)
    def _():
        o_ref[...] = (ht_ref[...] + acc_ref[...]).T


def _peer_dense(h2t, hn_t, u_bf, vt_bf, a0, n0, b1, e1):
    d, n = h2t.shape
    e = u_bf.shape[0]
    tt, te = PEER_TOKENS, PEER_EXPERT_TILE
    ipt = te // PEER_KEYS
    col = lambda i, j: (0, i)
    per_i = pl.BlockSpec((PEER_HEADS, ipt, tt), lambda i, j: (0, j, i))
    per_t = pl.BlockSpec((PEER_HEADS, PEER_KEYS, tt), lambda i, j: (0, 0, i))
    return pl.pallas_call(
        _peer_dense_kernel,
        grid=(n // tt, e // te),
        in_specs=[pl.BlockSpec((d, tt), col), pl.BlockSpec((d, tt), col),
                  pl.BlockSpec((te, d), lambda i, j: (j, 0)),
                  pl.BlockSpec((d, te), lambda i, j: (0, j)),
                  per_i, per_i, per_t, per_t],
        out_specs=pl.BlockSpec((tt, d), lambda i, j: (i, 0)),
        out_shape=jax.ShapeDtypeStruct((n, d), F32),
        scratch_shapes=[pltpu.VMEM((d, tt), F32)]
        + [pltpu.VMEM((PEER_DENSE_CHUNK, tt), F32)] * 2 + [pltpu.VMEM((PEER_DENSE_CHUNK, tt), BF16)] * 2,
        compiler_params=_cparams(("parallel", "arbitrary")),
        name="peer_dense",
    )(h2t, hn_t, u_bf, vt_bf, a0, n0, b1, e1)


def _layer(h, mem, p):
    bsz, seq, d = h.shape
    n = bsz * seq
    x2 = h.reshape(n, d)
    row = lambda a: a.reshape(1, -1)

    seg = jnp.asarray(np.kron(np.eye(ATT_HEADS), np.ones((ATT_HEAD_DIM, ATT_HEAD_DIM))), BF16)
    q, k, v, u = _inproj(
        x2, row(p["norm_mix_g"]), p["w_in"].astype(BF16), seg,
        row(jnp.tile(p["att_q_g"], ATT_HEADS)),
        row(jnp.tile(p["att_k_g"], ATT_HEADS)))
    shp = (bsz, seq, ATT_WIDTH)
    att = _attention(q.reshape(shp), k.reshape(shp), v.reshape(shp),
                     _band_bias_tile(p["rel_bias"]), bsz, seq).reshape(n, ATT_WIDTH)

    t, g, c = SSM_T, SSM_GROUPS, SSM_GROUP
    chunks = seq // t
    n_steps = max(1, int(math.ceil(math.log2(chunks))))
    k_lag, bcat, ccat, apr, api = _ssm_params(
        p["ssm_lam_re"], p["ssm_lam_im"], p["ssm_log_step"], p["ssm_b_re"], p["ssm_b_im"],
        p["ssm_c_re"], p["ssm_c_im"], n_steps)
    u_g = u.reshape(bsz, chunks, t, g, c).transpose(3, 0, 1, 2, 4).reshape(g, bsz * chunks, t * c)
    d_flat = jnp.tile(p["ssm_d"], (1, t))[:, None, :]
    y_g = _ssm_main(u_g, k_lag, bcat, ccat, apr, api, d_flat, chunks, n_steps)
    y = y_g.reshape(g, bsz, chunks, t, c).transpose(1, 2, 3, 0, 4).reshape(n, SSM_WIDTH)

    w_out = p["w_out"].astype(BF16)
    h1 = _mix_out(x2, att, y, p["ssm_w_glu"].astype(BF16), row(p["ssm_b_glu"]),
                  row(p["att_out_g"]), row(p["ssm_out_g"]), w_out[:ATT_WIDTH], w_out[ATT_WIDTH:])

    kn, vm = _mem_kv(mem, row(p["norm_memkv_g"]), p["w_mem_k"].astype(BF16),
                     p["w_mem_v"].astype(BF16), row(p["mem_k_g"]))
    h2t = _mem_attn(h1.reshape(bsz, seq, d), row(p["norm_mem_g"]), p["w_mem_q"].astype(BF16),
                    row(p["mem_q_g"]), kn, vm, p["w_mem_o"].astype(BF16), bsz, seq)

    keys_bf = p["peer_keys"].reshape(PEER_HEADS * 2, PEER_KEYS, -1).astype(BF16)
    hn_t, a0, n0, b1, e1 = _peer_select(h2t, p["norm_peer_g"].reshape(d, 1),
                                        p["w_peer_q"].T.astype(BF16), keys_bf)
    h3 = _peer_dense(h2t, hn_t, p["peer_u"].astype(BF16), p["peer_v"].T.astype(BF16),
                     a0, n0, b1, e1)
    return h3.reshape(bsz, seq, d)


_PARAM_NAMES = (
    "norm_mix_g", "w_in", "att_q_g", "att_k_g", "rel_bias", "ssm_lam_re", "ssm_lam_im",
    "ssm_log_step", "ssm_b_re", "ssm_b_im", "ssm_c_re", "ssm_c_im", "ssm_d", "ssm_w_glu",
    "ssm_b_glu", "att_out_g", "ssm_out_g", "w_out", "norm_mem_g", "norm_memkv_g", "w_mem_q",
    "w_mem_k", "w_mem_v", "mem_q_g", "mem_k_g", "w_mem_o", "norm_peer_g", "w_peer_q",
    "peer_keys", "peer_u", "peer_v")


def kernel(x, mem, norm_mix_g, w_in, att_q_g, att_k_g, rel_bias, ssm_lam_re, ssm_lam_im, ssm_log_step, ssm_b_re, ssm_b_im, ssm_c_re, ssm_c_im, ssm_d, ssm_w_glu, ssm_b_glu, att_out_g, ssm_out_g, w_out, norm_mem_g, norm_memkv_g, w_mem_q, w_mem_k, w_mem_v, mem_q_g, mem_k_g, w_mem_o, norm_peer_g, w_peer_q, peer_keys, peer_u, peer_v):
    stacked = (norm_mix_g, w_in, att_q_g, att_k_g, rel_bias, ssm_lam_re, ssm_lam_im, ssm_log_step,
               ssm_b_re, ssm_b_im, ssm_c_re, ssm_c_im, ssm_d, ssm_w_glu, ssm_b_glu, att_out_g,
               ssm_out_g, w_out, norm_mem_g, norm_memkv_g, w_mem_q, w_mem_k, w_mem_v, mem_q_g,
               mem_k_g, w_mem_o, norm_peer_g, w_peer_q, peer_keys, peer_u, peer_v)
    h = x
    for l in range(norm_mix_g.shape[0]):
        h = _layer(h, mem, {name: a[l] for name, a in zip(_PARAM_NAMES, stacked)})
    return h
```

```python
import functools
import math

import numpy as np
import jax
import jax.numpy as jnp
from jax import lax
from jax.experimental import pallas as pl
from jax.experimental.pallas import tpu as pltpu

F32 = jnp.float32
BF16 = jnp.bfloat16

EPS = 1e-6
NEG_INF = -1e30

CHUNK = 64
ATT_HEADS = 8
ATT_HEAD_DIM = 64
ATT_WIDTH = ATT_HEADS * ATT_HEAD_DIM
LEFT_CHUNKS = 8
REL_MAX = 128
SSM_GROUP = 16
SSM_GROUPS = 32
SSM_STATE = 64
SSM_WIDTH = SSM_GROUP * SSM_GROUPS
MEM_HEADS = 4
PEER_HEADS = 8
PEER_KEYS = 128
PEER_TOPK = 16

INPROJ_ROWS = 512
ATT_QBLOCK = 512
ATT_GROUP = 256
ATT_WINDOW = ATT_GROUP + LEFT_CHUNKS * CHUNK
SSM_T = 32
MIX_ROWS = 512
MEM_ROWS = 512
PEER_SEL_TOKENS = 256
PEER_TOKENS = 512
PEER_EXPERT_TILE = 4096
PEER_DENSE_CHUNK = 256
PEER_GATE_ROWS = 16

VMEM_LIMIT = 48 * 1024 * 1024
VMEM_LIMIT_PEER_DENSE = 56 * 1024 * 1024


def _cparams(sem, vmem_limit=VMEM_LIMIT):
    return pltpu.CompilerParams(dimension_semantics=sem, vmem_limit_bytes=vmem_limit)


def _rms_rows(x, gain):
    ms = jnp.mean(x * x, axis=-1, keepdims=True)
    return x * lax.rsqrt(ms + EPS) * gain


def _dot(a, b):
    return jnp.dot(a, b, preferred_element_type=F32)


def _dot_nt(a, b):
    return lax.dot_general(a, b, (((1,), (1,)), ((), ())), preferred_element_type=F32)


def _inproj_kernel(x_ref, g_ref, w_ref, seg_ref, qg_ref, kg_ref, q_ref, k_ref, v_ref, u_ref):
    xn = _rms_rows(x_ref[...], g_ref[...])
    proj = _dot(xn.astype(BF16), w_ref[...])
    seg = seg_ref[...]

    def head_norm(z, gain):
        zz = z * z
        hi = zz.astype(BF16)
        lo = (zz - hi.astype(F32)).astype(BF16)
        ss = _dot(hi, seg) + _dot(lo, seg)
        return z * lax.rsqrt(ss * (1.0 / ATT_HEAD_DIM) + EPS) * gain

    q = head_norm(proj[:, :ATT_WIDTH], qg_ref[...]) * (ATT_HEAD_DIM ** -0.5)
    q_ref[...] = q.astype(BF16)
    k_ref[...] = head_norm(proj[:, ATT_WIDTH:2 * ATT_WIDTH], kg_ref[...]).astype(BF16)
    v_ref[...] = proj[:, 2 * ATT_WIDTH:3 * ATT_WIDTH].astype(BF16)
    u_ref[...] = proj[:, 3 * ATT_WIDTH:]


def _inproj(x2, g, w_in_bf, seg, qg, kg):
    n, d = x2.shape
    tm = INPROJ_ROWS
    row = lambda i: (i, 0)
    fix = lambda i: (0, 0)
    outw = ATT_WIDTH
    return pl.pallas_call(
        _inproj_kernel,
        grid=(n // tm,),
        in_specs=[pl.BlockSpec((tm, d), row), pl.BlockSpec((1, d), fix),
                  pl.BlockSpec(w_in_bf.shape, fix), pl.BlockSpec(seg.shape, fix),
                  pl.BlockSpec((1, outw), fix), pl.BlockSpec((1, outw), fix)],
        out_specs=[pl.BlockSpec((tm, outw), row)] * 3 + [pl.BlockSpec((tm, SSM_WIDTH), row)],
        out_shape=[jax.ShapeDtypeStruct((n, outw), BF16)] * 3
        + [jax.ShapeDtypeStruct((n, SSM_WIDTH), F32)],
        compiler_params=_cparams(("parallel",)),
        name="inproj",
    )(x2, g, w_in_bf, seg, qg, kg)


def _attn_kernel(q_ref, kp_ref, kc_ref, vp_ref, vc_ref, bias_ref, o_ref):
    blk = pl.program_id(1)
    kk = jnp.concatenate([kp_ref[...], kc_ref[...]], axis=0)
    vv = jnp.concatenate([vp_ref[...], vc_ref[...]], axis=0)
    chunks_per_block = ATT_QBLOCK // CHUNK
    key_chunk = lax.broadcasted_iota(jnp.int32, (1, ATT_WINDOW), 1) // CHUNK
    for gi in range(ATT_QBLOCK // ATT_GROUP):
        r0 = gi * ATT_GROUP
        q = q_ref[r0:r0 + ATT_GROUP, :]
        kwin = kk[r0:r0 + ATT_WINDOW, :]
        vwin = vv[r0:r0 + ATT_WINDOW, :]
        abs_chunk = blk * chunks_per_block + gi * (ATT_GROUP // CHUNK) + key_chunk - LEFT_CHUNKS
        valid = abs_chunk >= 0
        for h in range(ATT_HEADS):
            c0 = h * ATT_HEAD_DIM
            s = _dot_nt(q[:, c0:c0 + ATT_HEAD_DIM], kwin[:, c0:c0 + ATT_HEAD_DIM])
            s = jnp.where(valid, s + bias_ref[h], NEG_INF)
            m = jnp.max(s, axis=-1, keepdims=True)
            p = jnp.exp(s - m)
            l = jnp.sum(p, axis=-1, keepdims=True)
            o = _dot(p.astype(BF16), vwin[:, c0:c0 + ATT_HEAD_DIM])
            o_ref[r0:r0 + ATT_GROUP, c0:c0 + ATT_HEAD_DIM] = o / l


def _attention(q, k, v, bias, bsz, seq):
    qb = ATT_QBLOCK
    cur = lambda b, i: (b, i, 0)
    prev = lambda b, i: (b, jnp.maximum(i - 1, 0), 0)
    blk = (None, qb, ATT_WIDTH)
    return pl.pallas_call(
        _attn_kernel,
        grid=(bsz, seq // qb),
        in_specs=[pl.BlockSpec(blk, cur), pl.BlockSpec(blk, prev), pl.BlockSpec(blk, cur),
                  pl.BlockSpec(blk, prev), pl.BlockSpec(blk, cur),
                  pl.BlockSpec(bias.shape, lambda b, i: (0, 0, 0))],
        out_specs=pl.BlockSpec(blk, cur),
        out_shape=jax.ShapeDtypeStruct((bsz, seq, ATT_WIDTH), F32),
        compiler_params=_cparams(("parallel", "parallel")),
        name="attn",
    )(q, k, k, v, v, bias)


def _band_bias_tile(rel_bias):
    heads = rel_bias.shape[0]
    period = 1024
    delta = (np.arange(period) + (ATT_GROUP - 1)) % period - (ATT_GROUP - 1)
    bucket = np.clip(LEFT_CHUNKS * CHUNK - delta, -(CHUNK - 1), REL_MAX) + (CHUNK - 1)
    vec = rel_bias[:, bucket]
    flat = jnp.tile(vec, (1, ATT_GROUP))[:, :ATT_GROUP * (period - 1)]
    tile = flat.reshape(heads, ATT_GROUP, period - 1)[:, :, :ATT_WINDOW]
    r = np.arange(ATT_GROUP)[:, None]
    s = np.arange(ATT_WINDOW)[None, :]
    qc, kc = r // CHUNK, s // CHUNK
    band = (kc >= qc) & (kc <= qc + LEFT_CHUNKS)
    return jnp.where(band[None], tile, NEG_INF)


def _ssm_param_kernel(lr_r_ref, li_r_ref, lr_c_ref, li_c_ref, ls_ref, taus_ref,
                      brt_ref, bit_ref, crt_ref, cit_ref,
                      k_ref, bcat_ref, ccat_ref, apr_ref, api_ref):
    t = SSM_T
    p = SSM_STATE
    step = jnp.exp(ls_ref[...])
    lr_r, li_r = lr_r_ref[...], li_r_ref[...]
    lr_c, li_c = lr_c_ref[...], li_c_ref[...]

    def zoh(lr, li):
        mag = jnp.exp(lr * step)
        ar = mag * jnp.cos(li * step)
        ai = mag * jnp.sin(li * step)
        zr = ar - 1.0
        den = lr * lr + li * li
        return (zr * lr + ai * li) / den, (ai * lr - zr * li) / den

    def power(lr, li, tau):
        mag = jnp.exp(lr * step * tau)
        ph = li * step * tau
        return mag * jnp.cos(ph), mag * jnp.sin(ph)

    fr_r, fi_r = zoh(lr_r, li_r)
    brt, bit = brt_ref[...], bit_ref[...]
    bbr_t = fr_r * brt - fi_r * bit
    bbi_t = fr_r * bit + fi_r * brt
    crt, cit = crt_ref[...], cit_ref[...]
    lane_t = (lax.broadcasted_iota(jnp.int32, (1, t * SSM_GROUP), 1) // SSM_GROUP).astype(F32)

    p0_r, p0_i = power(lr_c, li_c, lane_t)
    hp = lax.Precision.HIGHEST
    k_ref[...] = (jnp.dot(bbr_t, crt * p0_r - cit * p0_i, precision=hp, preferred_element_type=F32)
                  - jnp.dot(bbi_t, crt * p0_i + cit * p0_r, precision=hp, preferred_element_type=F32))

    tau = lax.broadcasted_iota(jnp.int32, (t, 1), 0).astype(F32)
    rv_r, rv_i = power(lr_r, li_r, float(t - 1) - tau)
    b_re = rv_r[:, None, :] * bbr_t[None] - rv_i[:, None, :] * bbi_t[None]
    b_im = rv_r[:, None, :] * bbi_t[None] + rv_i[:, None, :] * bbr_t[None]
    bcat_ref[:, 0:p] = b_re.reshape(t * SSM_GROUP, p)
    bcat_ref[:, p:2 * p] = b_im.reshape(t * SSM_GROUP, p)

    p1_r, p1_i = power(lr_c, li_c, lane_t + 1.0)
    ccat_ref[0:p, :] = crt * p1_r - cit * p1_i
    ccat_ref[p:2 * p, :] = -(crt * p1_i + cit * p1_r)

    dr, di = power(lr_r, li_r, taus_ref[...])
    apr_ref[:, 0:p] = dr
    apr_ref[:, p:2 * p] = dr
    api_ref[:, 0:p] = -di
    api_ref[:, p:2 * p] = di


def _ssm_params(lam_re, lam_im, log_step, b_re, b_im, c_re, c_im, n_steps):
    g, p, c = SSM_GROUPS, SSM_STATE, SSM_GROUP
    t = SSM_T
    taus = jnp.asarray([[float(t * 2 ** k)] for k in range(n_steps)], F32)
    c_re_t = jnp.swapaxes(c_re, 1, 2)
    c_im_t = jnp.swapaxes(c_im, 1, 2)
    args = [
        lam_re[:, None, :], lam_im[:, None, :], lam_re[:, :, None], lam_im[:, :, None],
        log_step[:, None, None], taus,
        jnp.swapaxes(b_re, 1, 2), jnp.swapaxes(b_im, 1, 2),
        jnp.tile(c_re_t, (1, 1, t)), jnp.tile(c_im_t, (1, 1, t)),
    ]

    def spec(a):
        if a.ndim == 2:
            return pl.BlockSpec(a.shape, lambda i: (0, 0))
        return pl.BlockSpec((None,) + a.shape[1:], lambda i: (i, 0, 0))

    out_dims = [(c, t * c), (t * c, 2 * p), (2 * p, t * c), (n_steps, 2 * p), (n_steps, 2 * p)]
    return pl.pallas_call(
        _ssm_param_kernel,
        grid=(g,),
        in_specs=[spec(a) for a in args],
        out_specs=[pl.BlockSpec((None,) + d, lambda i: (i, 0, 0)) for d in out_dims],
        out_shape=[jax.ShapeDtypeStruct((g,) + d, F32) for d in out_dims],
        compiler_params=_cparams(("parallel",)),
        name="ssm_params",
    )(*args)


def _ssm_main_kernel(u_ref, k_ref, bcat_ref, ccat_ref, apr_ref, api_ref, d_ref, y_ref, m_ref,
                     *, rows_per_batch, n_steps):
    k = k_ref[...]
    lane = lax.broadcasted_iota(jnp.int32, k.shape, 1)
    for s in range(SSM_T):
        sh = s * SSM_GROUP
        blk = k if s == 0 else jnp.where(lane >= sh, pltpu.roll(k, sh, 1), 0.0)
        m_ref[s * SSM_GROUP:(s + 1) * SSM_GROUP, :] = blk.astype(BF16)
    u = u_ref[...]
    ub = u.astype(BF16)
    y = _dot(ub, m_ref[...])
    x = _dot(ub, bcat_ref[...].astype(BF16))
    r = x.shape[0]
    k_in_batch = lax.broadcasted_iota(jnp.int32, (r, 1), 0) % rows_per_batch
    half = SSM_STATE
    for k in range(n_steps):
        d = 2 ** k
        sh = jnp.where(k_in_batch >= d, pltpu.roll(x, d, 0), 0.0)
        x = x + sh * apr_ref[k:k + 1, :] + pltpu.roll(sh, half, 1) * api_ref[k:k + 1, :]
    x_prev = jnp.where(k_in_batch >= 1, pltpu.roll(x, 1, 0), 0.0)
    y = y + _dot(x_prev.astype(BF16), ccat_ref[...].astype(BF16))
    y_ref[...] = y + u * d_ref[...]


def _ssm_main(u_g, k_lag, bcat, ccat, apr, api, d_flat, rows_per_batch, n_steps):
    g, r, w = u_g.shape
    per_g = lambda a: pl.BlockSpec((None,) + a.shape[1:], lambda i: (i, 0, 0))
    kern = functools.partial(_ssm_main_kernel, rows_per_batch=rows_per_batch, n_steps=n_steps)
    return pl.pallas_call(
        kern,
        grid=(g,),
        in_specs=[per_g(a) for a in (u_g, k_lag, bcat, ccat, apr, api, d_flat)],
        out_specs=per_g(u_g),
        out_shape=jax.ShapeDtypeStruct((g, r, w), F32),
        scratch_shapes=[pltpu.VMEM((w, w), BF16)],
        compiler_params=_cparams(("parallel",)),
        name="ssm_main",
    )(u_g, k_lag, bcat, ccat, apr, api, d_flat)


def _gelu(x):
    return 0.5 * x * (1.0 + lax.erf(x * (1.0 / math.sqrt(2.0))))


def _mix_out_kernel(x_ref, att_ref, y_ref, wg_ref, bg_ref, ag_ref, sg_ref, wa_ref, ws_ref, h_ref):
    y = _gelu(y_ref[...])
    z = _dot(y.astype(BF16), wg_ref[...]) + bg_ref[...]
    ssm = y * jax.nn.sigmoid(z)
    a = _rms_rows(att_ref[...], ag_ref[...])
    s = _rms_rows(ssm, sg_ref[...])
    h_ref[...] = x_ref[...] + _dot(a.astype(BF16), wa_ref[...]) + _dot(s.astype(BF16), ws_ref[...])


def _mix_out(x2, att, y, w_glu_bf, b_glu, att_g, ssm_g, w_out_a, w_out_s):
    n, d = x2.shape
    tm = MIX_ROWS
    row = lambda i: (i, 0)
    fix = lambda i: (0, 0)
    full = lambda a: pl.BlockSpec(a.shape, fix)
    return pl.pallas_call(
        _mix_out_kernel,
        grid=(n // tm,),
        in_specs=[pl.BlockSpec((tm, d), row), pl.BlockSpec((tm, ATT_WIDTH), row),
                  pl.BlockSpec((tm, SSM_WIDTH), row), full(w_glu_bf), full(b_glu), full(att_g),
                  full(ssm_g), full(w_out_a), full(w_out_s)],
        out_specs=pl.BlockSpec((tm, d), row),
        out_shape=jax.ShapeDtypeStruct((n, d), F32),
        compiler_params=_cparams(("parallel",)),
        name="mix_out",
    )(x2, att, y, w_glu_bf, b_glu, att_g, ssm_g, w_out_a, w_out_s)


def _mem_kv_kernel(mem_ref, g_ref, wk_ref, wv_ref, kg_ref, k_ref, v_ref):
    mn = _rms_rows(mem_ref[...], g_ref[...]).astype(BF16)
    k = _dot(mn, wk_ref[...])
    hd = k.shape[1] // MEM_HEADS
    for h in range(MEM_HEADS):
        k_ref[:, h * hd:(h + 1) * hd] = _rms_rows(k[:, h * hd:(h + 1) * hd], kg_ref[...]).astype(BF16)
    v_ref[...] = _dot(mn, wv_ref[...]).astype(BF16)


def _mem_kv(mem, g, wk_bf, wv_bf, kg):
    bsz, m, d = mem.shape
    fix = lambda b: (0, 0)
    per_b = pl.BlockSpec((None, m, d), lambda b: (b, 0, 0))
    return pl.pallas_call(
        _mem_kv_kernel,
        grid=(bsz,),
        in_specs=[per_b, pl.BlockSpec(g.shape, fix), pl.BlockSpec(wk_bf.shape, fix),
                  pl.BlockSpec(wv_bf.shape, fix), pl.BlockSpec(kg.shape, fix)],
        out_specs=[per_b, per_b],
        out_shape=[jax.ShapeDtypeStruct((bsz, m, d), BF16)] * 2,
        compiler_params=_cparams(("parallel",)),
        name="mem_kv",
    )(mem, g, wk_bf, wv_bf, kg)


def _mem_attn_kernel(h_ref, g_ref, wq_ref, qg_ref, k_ref, v_ref, wo_ref, o_ref):
    h1 = h_ref[...]
    hn = _rms_rows(h1, g_ref[...]).astype(BF16)
    q = _dot(hn, wq_ref[...])
    hd = q.shape[1] // MEM_HEADS
    outs = []
    for h in range(MEM_HEADS):
        sl = slice(h * hd, (h + 1) * hd)
        qh = (_rms_rows(q[:, sl], qg_ref[...]) * (hd ** -0.5)).astype(BF16)
        s = _dot_nt(qh, k_ref[:, sl])
        m = jnp.max(s, axis=-1, keepdims=True)
        p = jnp.exp(s - m)
        l = jnp.sum(p, axis=-1, keepdims=True)
        outs.append(_dot(p.astype(BF16), v_ref[:, sl]) / l)
    o = jnp.concatenate(outs, axis=-1).astype(BF16)
    o_ref[...] = (h1 + _dot(o, wo_ref[...])).T


def _mem_attn(h1, g, wq_bf, qg, kn, v, wo_bf, bsz, seq):
    d = h1.shape[-1]
    m = kn.shape[1]
    tm = MEM_ROWS
    steps = seq // tm
    fix = lambda b, i: (0, 0)
    tok = pl.BlockSpec((None, tm, d), lambda b, i: (b, i, 0))
    per_b = pl.BlockSpec((None, m, d), lambda b, i: (b, 0, 0))
    full = lambda a: pl.BlockSpec(a.shape, fix)
    return pl.pallas_call(
        _mem_attn_kernel,
        grid=(bsz, steps),
        in_specs=[tok, full(g), full(wq_bf), full(qg), per_b, per_b, full(wo_bf)],
        out_specs=pl.BlockSpec((d, tm), lambda b, i: (0, b * steps + i)),
        out_shape=jax.ShapeDtypeStruct((d, bsz * seq), F32),
        compiler_params=_cparams(("parallel", "parallel")),
        name="mem_attn",
    )(h1, g, wq_bf, qg, kn, v, wo_bf)


_CAND_ROWS = 80


def _top16_rows(scores, exact):
    rows = scores.shape[0]
    ridx = lax.broadcasted_iota(jnp.int32, scores.shape, 0).astype(F32) if exact else None
    work = scores
    rank = jnp.full(scores.shape, float(PEER_TOPK), F32)
    vals = []
    for r in range(PEER_TOPK):
        m = jnp.max(work, axis=0, keepdims=True)
        hit = work == m
        if exact:
            first = jnp.min(jnp.where(hit, ridx, float(rows)), axis=0, keepdims=True)
            hit = ridx == first
        rank = jnp.where(hit, float(r), rank)
        work = jnp.where(hit, -jnp.inf, work)
        vals.append(m)
    ranked = jnp.sum(jnp.where(rank < float(PEER_TOPK), 1.0, 0.0), axis=0, keepdims=True)
    return jnp.concatenate(vals, axis=0), rank, ranked


def _peer_head(keys_ref, qt_ref, h, exact):
    nk = PEER_KEYS
    halves = []
    for s in range(2):
        r0 = (2 * h + s) * nk
        st = _dot(keys_ref[2 * h + s], qt_ref[r0:r0 + nk, :].astype(BF16))
        halves.append((st,) + _top16_rows(st, exact))
    (s0, sv0, rank0, ranked0), (s1, sv1, rank1, ranked1) = halves
    pieces = [sv0[0:1] + sv1]
    pieces += [sv0[a:a + 1] + sv1[0:8] for a in range(1, 8)]
    pieces += [sv0[8:16] + sv1[0:1]]
    cand = jnp.concatenate(pieces, axis=0)
    ridx = lax.broadcasted_iota(jnp.int32, cand.shape, 0).astype(F32) if exact else None
    work = cand
    for _ in range(PEER_TOPK):
        m = jnp.max(work, axis=0, keepdims=True)
        hit = work == m
        if exact:
            first = jnp.min(jnp.where(hit, ridx, float(_CAND_ROWS)), axis=0, keepdims=True)
            hit = ridx == first
        work = jnp.where(hit, -jnp.inf, work)
    sel = jnp.where(work == -jnp.inf, 1.0, 0.0)
    chosen = jnp.sum(sel, axis=0, keepdims=True)
    z = jnp.sum(sel * jnp.exp(cand - cand[0:1]), axis=0, keepdims=True)
    counts = [jnp.sum(sel[0:16], axis=0, keepdims=True)]
    counts += [jnp.sum(sel[16 + 8 * (a - 1):24 + 8 * (a - 1)], axis=0, keepdims=True)
               for a in range(1, 8)]
    counts += [sel[72 + a:73 + a] for a in range(8)]
    n_dense = jnp.zeros(rank0.shape, F32)
    for a in range(PEER_TOPK):
        n_dense = jnp.where(rank0 == float(a), counts[a], n_dense)
    a_dense = jnp.exp(s0 - sv0[0:1]) / z
    e_dense = jnp.exp(s1 - sv1[0:1])
    k = float(PEER_TOPK)
    clean = (ranked0 == k) & (ranked1 == k) & (chosen == k)
    return a_dense, n_dense, rank1, e_dense, clean


def _peer_select_kernel(ht_ref, g_ref, wq_ref, keys_ref, hn_ref, a0_ref, n0_ref, b1_ref, e1_ref,
                        qt_ref):
    ht = ht_ref[...]
    ms = jnp.mean(ht * ht, axis=0, keepdims=True)
    hn = (ht * lax.rsqrt(ms + EPS) * g_ref[...]).astype(BF16)
    hn_ref[...] = hn
    qt_ref[...] = _dot(wq_ref[...], hn)

    def write(h, exact):
        a_dense, n_dense, rank1, e_dense, clean = _peer_head(keys_ref, qt_ref, h, exact)
        a0_ref[h] = a_dense
        n0_ref[h] = n_dense
        b1_ref[h] = rank1.astype(BF16)
        e1_ref[h] = e_dense.astype(BF16)
        return clean

    def redo(h):
        write(h, exact=True)

    tied = [jnp.max(jnp.where(write(h, exact=False), 0.0, 1.0)) for h in range(PEER_HEADS)]
    for h in range(PEER_HEADS):
        pl.when(tied[h] > 0.0)(functools.partial(redo, h))


def _peer_select(h2t, g_col, wq_t_bf, keys_bf):
    d, n = h2t.shape
    tt = PEER_SEL_TOKENS
    col = lambda i: (0, i)
    dense = pl.BlockSpec((PEER_HEADS, PEER_KEYS, tt), lambda i: (0, 0, i))
    dense_shape = jax.ShapeDtypeStruct((PEER_HEADS, PEER_KEYS, n), BF16)
    per_key = dense
    per_key_shape = jax.ShapeDtypeStruct((PEER_HEADS, PEER_KEYS, n), F32)
    return pl.pallas_call(
        _peer_select_kernel,
        grid=(n // tt,),
        in_specs=[pl.BlockSpec((d, tt), col), pl.BlockSpec(g_col.shape, lambda i: (0, 0)),
                  pl.BlockSpec(wq_t_bf.shape, lambda i: (0, 0)),
                  pl.BlockSpec(keys_bf.shape, lambda i: (0, 0, 0))],
        out_specs=[pl.BlockSpec((d, tt), col), per_key, per_key, dense, dense],
        out_shape=[jax.ShapeDtypeStruct((d, n), BF16), per_key_shape, per_key_shape, dense_shape,
                   dense_shape],
        scratch_shapes=[pltpu.VMEM((wq_t_bf.shape[0], tt), F32)],
        compiler_params=_cparams(("parallel",)),
        name="peer_select",
    )(h2t, g_col, wq_t_bf, keys_bf)


def _peer_dense_kernel(ht_ref, hn_ref, u_ref, vt_ref, a0_ref, n0_ref, b1_ref, e1_ref, o_ref,
                       acc_ref, act0_ref, act1_ref, y0_ref, y1_ref):
    j = pl.program_id(1)

    @pl.when(j == 0)
    def _():
        acc_ref[...] = jnp.zeros_like(acc_ref)

    nk = PEER_KEYS
    kc = PEER_DENSE_CHUNK
    n_chunks = PEER_EXPERT_TILE // kc
    act_refs = (act0_ref, act1_ref)
    y_refs = (y0_ref, y1_ref)

    def activations(c):
        act_refs[c % 2][...] = _dot(u_ref[c * kc:(c + 1) * kc, :], hn_ref[...])

    def gate(c):
        act_ref, y_ref = act_refs[c % 2], y_refs[c % 2]
        gr = PEER_GATE_ROWS
        tokens = act_ref.shape[1]
        for ii in range(kc // nk):
            i = c * (kc // nk) + ii
            w = [None] * (nk // gr)
            for h in range(PEER_HEADS):
                n_i = jnp.broadcast_to(n0_ref[h, i:i + 1, :], (gr, tokens)).astype(BF16)
                a_i = jnp.broadcast_to(a0_ref[h, i:i + 1, :], (gr, tokens)).astype(BF16)
                for jb in range(nk // gr):
                    js = slice(jb * gr, (jb + 1) * gr)
                    term = a_i * jnp.where(b1_ref[h, js, :] < n_i, e1_ref[h, js, :], 0.0)
                    w[jb] = term if w[jb] is None else w[jb] + term
            for jb in range(nk // gr):
                rows = slice(ii * nk + jb * gr, ii * nk + (jb + 1) * gr)
                y_ref[rows, :] = w[jb] * _gelu(act_ref[rows, :]).astype(BF16)

    def project(c):
        acc_ref[...] += _dot(vt_ref[:, c * kc:(c + 1) * kc], y_refs[c % 2][...])

    activations(0)
    for c in range(n_chunks):
        if c + 1 < n_chunks:
            activations(c + 1)
        gate(c)
        if c >= 1:
            project(c - 1)
    project(n_chunks - 1)

    @pl.when(j == pl.num_programs(1) - 1)
    def _():
        o_ref[...] = (ht_ref[...] + acc_ref[...]).T


def _peer_dense(h2t, hn_t, u_bf, vt_bf, a0, n0, b1, e1):
    d, n = h2t.shape
    e = u_bf.shape[0]
    tt, te = PEER_TOKENS, PEER_EXPERT_TILE
    ipt = te // PEER_KEYS
    col = lambda i, j: (0, i)
    per_i = pl.BlockSpec((PEER_HEADS, ipt, tt), lambda i, j: (0, j, i))
    per_t = pl.BlockSpec((PEER_HEADS, PEER_KEYS, tt), lambda i, j: (0, 0, i))
    return pl.pallas_call(
        _peer_dense_kernel,
        grid=(n // tt, e // te),
        in_specs=[pl.BlockSpec((d, tt), col), pl.BlockSpec((d, tt), col),
                  pl.BlockSpec((te, d), lambda i, j: (j, 0)),
                  pl.BlockSpec((d, te), lambda i, j: (0, j)),
                  per_i, per_i, per_t, per_t],
        out_specs=pl.BlockSpec((tt, d), lambda i, j: (i, 0)),
        out_shape=jax.ShapeDtypeStruct((n, d), F32),
        scratch_shapes=[pltpu.VMEM((d, tt), F32)]
        + [pltpu.VMEM((PEER_DENSE_CHUNK, tt), F32)] * 2 + [pltpu.VMEM((PEER_DENSE_CHUNK, tt), BF16)] * 2,
        compiler_params=_cparams(("parallel", "arbitrary"), VMEM_LIMIT_PEER_DENSE),
        name="peer_dense",
    )(h2t, hn_t, u_bf, vt_bf, a0, n0, b1, e1)


def _layer(h, mem, p):
    bsz, seq, d = h.shape
    n = bsz * seq
    x2 = h.reshape(n, d)
    row = lambda a: a.reshape(1, -1)

    seg = jnp.asarray(np.kron(np.eye(ATT_HEADS), np.ones((ATT_HEAD_DIM, ATT_HEAD_DIM))), BF16)
    q, k, v, u = _inproj(
        x2, row(p["norm_mix_g"]), p["w_in"].astype(BF16), seg,
        row(jnp.tile(p["att_q_g"], ATT_HEADS)),
        row(jnp.tile(p["att_k_g"], ATT_HEADS)))
    shp = (bsz, seq, ATT_WIDTH)
    att = _attention(q.reshape(shp), k.reshape(shp), v.reshape(shp),
                     _band_bias_tile(p["rel_bias"]), bsz, seq).reshape(n, ATT_WIDTH)

    t, g, c = SSM_T, SSM_GROUPS, SSM_GROUP
    chunks = seq // t
    n_steps = max(1, int(math.ceil(math.log2(chunks))))
    k_lag, bcat, ccat, apr, api = _ssm_params(
        p["ssm_lam_re"], p["ssm_lam_im"], p["ssm_log_step"], p["ssm_b_re"], p["ssm_b_im"],
        p["ssm_c_re"], p["ssm_c_im"], n_steps)
    u_g = u.reshape(bsz, chunks, t, g, c).transpose(3, 0, 1, 2, 4).reshape(g, bsz * chunks, t * c)
    d_flat = jnp.tile(p["ssm_d"], (1, t))[:, None, :]
    y_g = _ssm_main(u_g, k_lag, bcat, ccat, apr, api, d_flat, chunks, n_steps)
    y = y_g.reshape(g, bsz, chunks, t, c).transpose(1, 2, 3, 0, 4).reshape(n, SSM_WIDTH)

    w_out = p["w_out"].astype(BF16)
    h1 = _mix_out(x2, att, y, p["ssm_w_glu"].astype(BF16), row(p["ssm_b_glu"]),
                  row(p["att_out_g"]), row(p["ssm_out_g"]), w_out[:ATT_WIDTH], w_out[ATT_WIDTH:])

    kn, vm = _mem_kv(mem, row(p["norm_memkv_g"]), p["w_mem_k"].astype(BF16),
                     p["w_mem_v"].astype(BF16), row(p["mem_k_g"]))
    h2t = _mem_attn(h1.reshape(bsz, seq, d), row(p["norm_mem_g"]), p["w_mem_q"].astype(BF16),
                    row(p["mem_q_g"]), kn, vm, p["w_mem_o"].astype(BF16), bsz, seq)

    keys_bf = p["peer_keys"].reshape(PEER_HEADS * 2, PEER_KEYS, -1).astype(BF16)
    hn_t, a0, n0, b1, e1 = _peer_select(h2t, p["norm_peer_g"].reshape(d, 1),
                                        p["w_peer_q"].T.astype(BF16), keys_bf)
    h3 = _peer_dense(h2t, hn_t, p["peer_u"].astype(BF16), p["peer_v"].T.astype(BF16),
                     a0, n0, b1, e1)
    return h3.reshape(bsz, seq, d)


_PARAM_NAMES = (
    "norm_mix_g", "w_in", "att_q_g", "att_k_g", "rel_bias", "ssm_lam_re", "ssm_lam_im",
    "ssm_log_step", "ssm_b_re", "ssm_b_im", "ssm_c_re", "ssm_c_im", "ssm_d", "ssm_w_glu",
    "ssm_b_glu", "att_out_g", "ssm_out_g", "w_out", "norm_mem_g", "norm_memkv_g", "w_mem_q",
    "w_mem_k", "w_mem_v", "mem_q_g", "mem_k_g", "w_mem_o", "norm_peer_g", "w_peer_q",
    "peer_keys", "peer_u", "peer_v")


def kernel(x, mem, norm_mix_g, w_in, att_q_g, att_k_g, rel_bias, ssm_lam_re, ssm_lam_im, ssm_log_step, ssm_b_re, ssm_b_im, ssm_c_re, ssm_c_im, ssm_d, ssm_w_glu, ssm_b_glu, att_out_g, ssm_out_g, w_out, norm_mem_g, norm_memkv_g, w_mem_q, w_mem_k, w_mem_v, mem_q_g, mem_k_g, w_mem_o, norm_peer_g, w_peer_q, peer_keys, peer_u, peer_v):
    stacked = (norm_mix_g, w_in, att_q_g, att_k_g, rel_bias, ssm_lam_re, ssm_lam_im, ssm_log_step,
               ssm_b_re, ssm_b_im, ssm_c_re, ssm_c_im, ssm_d, ssm_w_glu, ssm_b_glu, att_out_g,
               ssm_out_g, w_out, norm_mem_g, norm_memkv_g, w_mem_q, w_mem_k, w_mem_v, mem_q_g,
               mem_k_g, w_mem_o, norm_peer_g, w_peer_q, peer_keys, peer_u, peer_v)
    h = x
    for l in range(norm_mix_g.shape[0]):
        h = _layer(h, mem, {name: a[l] for name, a in zip(_PARAM_NAMES, stacked)})
    return h
```

```python
import functools
import math

import numpy as np
import jax
import jax.numpy as jnp
from jax import lax
from jax.experimental import pallas as pl
from jax.experimental.pallas import tpu as pltpu

F32 = jnp.float32
BF16 = jnp.bfloat16

EPS = 1e-6
NEG_INF = -1e30

CHUNK = 64
ATT_HEADS = 8
ATT_HEAD_DIM = 64
ATT_WIDTH = ATT_HEADS * ATT_HEAD_DIM
LEFT_CHUNKS = 8
REL_MAX = 128
SSM_GROUP = 16
SSM_GROUPS = 32
SSM_STATE = 64
SSM_WIDTH = SSM_GROUP * SSM_GROUPS
MEM_HEADS = 4
PEER_HEADS = 8
PEER_KEYS = 128
PEER_TOPK = 16

INPROJ_ROWS = 512
ATT_QBLOCK = 512
ATT_GROUP = 256
ATT_WINDOW = ATT_GROUP + LEFT_CHUNKS * CHUNK
SSM_T = 32
MEM_ROWS = 512
PEER_SEL_TOKENS = 256
PEER_TOKENS = 512
PEER_EXPERT_TILE = 2048
PEER_DENSE_CHUNK = 256
PEER_GATE_ROWS = 16

VMEM_LIMIT = 48 * 1024 * 1024


def _cparams(sem):
    return pltpu.CompilerParams(dimension_semantics=sem, vmem_limit_bytes=VMEM_LIMIT)


def _rms_rows(x, gain):
    ms = jnp.mean(x * x, axis=-1, keepdims=True)
    return x * lax.rsqrt(ms + EPS) * gain


def _dot(a, b):
    return jnp.dot(a, b, preferred_element_type=F32)


def _dot_nt(a, b):
    return lax.dot_general(a, b, (((1,), (1,)), ((), ())), preferred_element_type=F32)


def _inproj_kernel(x_ref, g_ref, w_ref, seg_ref, qg_ref, kg_ref, q_ref, k_ref, v_ref, u_ref):
    xn = _rms_rows(x_ref[...], g_ref[...])
    proj = _dot(xn.astype(BF16), w_ref[...])
    seg = seg_ref[...]

    def head_norm(z, gain):
        zz = z * z
        hi = zz.astype(BF16)
        lo = (zz - hi.astype(F32)).astype(BF16)
        ss = _dot(hi, seg) + _dot(lo, seg)
        return z * lax.rsqrt(ss * (1.0 / ATT_HEAD_DIM) + EPS) * gain

    q = head_norm(proj[:, :ATT_WIDTH], qg_ref[...]) * (ATT_HEAD_DIM ** -0.5)
    q_ref[...] = q.astype(BF16)
    k_ref[...] = head_norm(proj[:, ATT_WIDTH:2 * ATT_WIDTH], kg_ref[...]).astype(BF16)
    v_ref[...] = proj[:, 2 * ATT_WIDTH:3 * ATT_WIDTH].astype(BF16)
    u_ref[...] = proj[:, 3 * ATT_WIDTH:]


def _inproj(x2, g, w_in_bf, seg, qg, kg):
    n, d = x2.shape
    tm = INPROJ_ROWS
    row = lambda i: (i, 0)
    fix = lambda i: (0, 0)
    outw = ATT_WIDTH
    return pl.pallas_call(
        _inproj_kernel,
        grid=(n // tm,),
        in_specs=[pl.BlockSpec((tm, d), row), pl.BlockSpec((1, d), fix),
                  pl.BlockSpec(w_in_bf.shape, fix), pl.BlockSpec(seg.shape, fix),
                  pl.BlockSpec((1, outw), fix), pl.BlockSpec((1, outw), fix)],
        out_specs=[pl.BlockSpec((tm, outw), row)] * 3 + [pl.BlockSpec((tm, SSM_WIDTH), row)],
        out_shape=[jax.ShapeDtypeStruct((n, outw), BF16)] * 3
        + [jax.ShapeDtypeStruct((n, SSM_WIDTH), F32)],
        compiler_params=_cparams(("parallel",)),
        name="inproj",
    )(x2, g, w_in_bf, seg, qg, kg)


def _attn_kernel(q_ref, kp_ref, kc_ref, vp_ref, vc_ref, bias_ref, o_ref):
    blk = pl.program_id(1)
    kk = jnp.concatenate([kp_ref[...], kc_ref[...]], axis=0)
    vv = jnp.concatenate([vp_ref[...], vc_ref[...]], axis=0)
    chunks_per_block = ATT_QBLOCK // CHUNK
    key_chunk = lax.broadcasted_iota(jnp.int32, (1, ATT_WINDOW), 1) // CHUNK
    for gi in range(ATT_QBLOCK // ATT_GROUP):
        r0 = gi * ATT_GROUP
        q = q_ref[r0:r0 + ATT_GROUP, :]
        kwin = kk[r0:r0 + ATT_WINDOW, :]
        vwin = vv[r0:r0 + ATT_WINDOW, :]
        abs_chunk = blk * chunks_per_block + gi * (ATT_GROUP // CHUNK) + key_chunk - LEFT_CHUNKS
        valid = abs_chunk >= 0
        for h in range(ATT_HEADS):
            c0 = h * ATT_HEAD_DIM
            s = _dot_nt(q[:, c0:c0 + ATT_HEAD_DIM], kwin[:, c0:c0 + ATT_HEAD_DIM])
            s = jnp.where(valid, s + bias_ref[h], NEG_INF)
            m = jnp.max(s, axis=-1, keepdims=True)
            p = jnp.exp(s - m)
            l = jnp.sum(p, axis=-1, keepdims=True)
            o = _dot(p.astype(BF16), vwin[:, c0:c0 + ATT_HEAD_DIM])
            o_ref[r0:r0 + ATT_GROUP, c0:c0 + ATT_HEAD_DIM] = o / l


def _attention(q, k, v, bias, bsz, seq):
    qb = ATT_QBLOCK
    cur = lambda b, i: (b, i, 0)
    prev = lambda b, i: (b, jnp.maximum(i - 1, 0), 0)
    blk = (None, qb, ATT_WIDTH)
    return pl.pallas_call(
        _attn_kernel,
        grid=(bsz, seq // qb),
        in_specs=[pl.BlockSpec(blk, cur), pl.BlockSpec(blk, prev), pl.BlockSpec(blk, cur),
                  pl.BlockSpec(blk, prev), pl.BlockSpec(blk, cur),
                  pl.BlockSpec(bias.shape, lambda b, i: (0, 0, 0))],
        out_specs=pl.BlockSpec(blk, cur),
        out_shape=jax.ShapeDtypeStruct((bsz, seq, ATT_WIDTH), F32),
        compiler_params=_cparams(("parallel", "parallel")),
        name="attn",
    )(q, k, k, v, v, bias)


def _band_bias_tile(rel_bias):
    heads = rel_bias.shape[0]
    period = 1024
    delta = (np.arange(period) + (ATT_GROUP - 1)) % period - (ATT_GROUP - 1)
    bucket = np.clip(LEFT_CHUNKS * CHUNK - delta, -(CHUNK - 1), REL_MAX) + (CHUNK - 1)
    vec = rel_bias[:, bucket]
    flat = jnp.tile(vec, (1, ATT_GROUP))[:, :ATT_GROUP * (period - 1)]
    tile = flat.reshape(heads, ATT_GROUP, period - 1)[:, :, :ATT_WINDOW]
    r = np.arange(ATT_GROUP)[:, None]
    s = np.arange(ATT_WINDOW)[None, :]
    qc, kc = r // CHUNK, s // CHUNK
    band = (kc >= qc) & (kc <= qc + LEFT_CHUNKS)
    return jnp.where(band[None], tile, NEG_INF)


def _ssm_param_kernel(lr_r_ref, li_r_ref, lr_c_ref, li_c_ref, ls_ref, taus_ref,
                      brt_ref, bit_ref, crt_ref, cit_ref,
                      k_ref, bcat_ref, ccat_ref, apr_ref, api_ref):
    t = SSM_T
    p = SSM_STATE
    step = jnp.exp(ls_ref[...])
    lr_r, li_r = lr_r_ref[...], li_r_ref[...]
    lr_c, li_c = lr_c_ref[...], li_c_ref[...]

    def zoh(lr, li):
        mag = jnp.exp(lr * step)
        ar = mag * jnp.cos(li * step)
        ai = mag * jnp.sin(li * step)
        zr = ar - 1.0
        den = lr * lr + li * li
        return (zr * lr + ai * li) / den, (ai * lr - zr * li) / den

    def power(lr, li, tau):
        mag = jnp.exp(lr * step * tau)
        ph = li * step * tau
        return mag * jnp.cos(ph), mag * jnp.sin(ph)

    fr_r, fi_r = zoh(lr_r, li_r)
    brt, bit = brt_ref[...], bit_ref[...]
    bbr_t = fr_r * brt - fi_r * bit
    bbi_t = fr_r * bit + fi_r * brt
    crt, cit = crt_ref[...], cit_ref[...]
    lane_t = (lax.broadcasted_iota(jnp.int32, (1, t * SSM_GROUP), 1) // SSM_GROUP).astype(F32)

    p0_r, p0_i = power(lr_c, li_c, lane_t)
    hp = lax.Precision.HIGHEST
    k_ref[...] = (jnp.dot(bbr_t, crt * p0_r - cit * p0_i, precision=hp, preferred_element_type=F32)
                  - jnp.dot(bbi_t, crt * p0_i + cit * p0_r, precision=hp, preferred_element_type=F32))

    tau = lax.broadcasted_iota(jnp.int32, (t, 1), 0).astype(F32)
    rv_r, rv_i = power(lr_r, li_r, float(t - 1) - tau)
    b_re = rv_r[:, None, :] * bbr_t[None] - rv_i[:, None, :] * bbi_t[None]
    b_im = rv_r[:, None, :] * bbi_t[None] + rv_i[:, None, :] * bbr_t[None]
    bcat_ref[:, 0:p] = b_re.reshape(t * SSM_GROUP, p)
    bcat_ref[:, p:2 * p] = b_im.reshape(t * SSM_GROUP, p)

    p1_r, p1_i = power(lr_c, li_c, lane_t + 1.0)
    ccat_ref[0:p, :] = crt * p1_r - cit * p1_i
    ccat_ref[p:2 * p, :] = -(crt * p1_i + cit * p1_r)

    dr, di = power(lr_r, li_r, taus_ref[...])
    apr_ref[:, 0:p] = dr
    apr_ref[:, p:2 * p] = dr
    api_ref[:, 0:p] = -di
    api_ref[:, p:2 * p] = di


def _ssm_params(lam_re, lam_im, log_step, b_re, b_im, c_re, c_im, n_steps):
    g, p, c = SSM_GROUPS, SSM_STATE, SSM_GROUP
    t = SSM_T
    taus = jnp.asarray([[float(t * 2 ** k)] for k in range(n_steps)], F32)
    c_re_t = jnp.swapaxes(c_re, 1, 2)
    c_im_t = jnp.swapaxes(c_im, 1, 2)
    args = [
        lam_re[:, None, :], lam_im[:, None, :], lam_re[:, :, None], lam_im[:, :, None],
        log_step[:, None, None], taus,
        jnp.swapaxes(b_re, 1, 2), jnp.swapaxes(b_im, 1, 2),
        jnp.tile(c_re_t, (1, 1, t)), jnp.tile(c_im_t, (1, 1, t)),
    ]

    def spec(a):
        if a.ndim == 2:
            return pl.BlockSpec(a.shape, lambda i: (0, 0))
        return pl.BlockSpec((None,) + a.shape[1:], lambda i: (i, 0, 0))

    out_dims = [(c, t * c), (t * c, 2 * p), (2 * p, t * c), (n_steps, 2 * p), (n_steps, 2 * p)]
    return pl.pallas_call(
        _ssm_param_kernel,
        grid=(g,),
        in_specs=[spec(a) for a in args],
        out_specs=[pl.BlockSpec((None,) + d, lambda i: (i, 0, 0)) for d in out_dims],
        out_shape=[jax.ShapeDtypeStruct((g,) + d, F32) for d in out_dims],
        compiler_params=_cparams(("parallel",)),
        name="ssm_params",
    )(*args)


def _ssm_main_kernel(u_ref, k_ref, bcat_ref, ccat_ref, apr_ref, api_ref, d_ref, y_ref, m_ref,
                     *, rows_per_batch, n_steps):
    k = k_ref[...]
    lane = lax.broadcasted_iota(jnp.int32, k.shape, 1)
    for s in range(SSM_T):
        sh = s * SSM_GROUP
        blk = k if s == 0 else jnp.where(lane >= sh, pltpu.roll(k, sh, 1), 0.0)
        m_ref[s * SSM_GROUP:(s + 1) * SSM_GROUP, :] = blk.astype(BF16)
    u = u_ref[...]
    ub = u.astype(BF16)
    y = _dot(ub, m_ref[...])
    x = _dot(ub, bcat_ref[...].astype(BF16))
    r = x.shape[0]
    k_in_batch = lax.broadcasted_iota(jnp.int32, (r, 1), 0) % rows_per_batch
    half = SSM_STATE
    for k in range(n_steps):
        d = 2 ** k
        sh = jnp.where(k_in_batch >= d, pltpu.roll(x, d, 0), 0.0)
        x = x + sh * apr_ref[k:k + 1, :] + pltpu.roll(sh, half, 1) * api_ref[k:k + 1, :]
    x_prev = jnp.where(k_in_batch >= 1, pltpu.roll(x, 1, 0), 0.0)
    y = y + _dot(x_prev.astype(BF16), ccat_ref[...].astype(BF16))
    y_ref[...] = y + u * d_ref[...]


def _ssm_main(u_g, k_lag, bcat, ccat, apr, api, d_flat, rows_per_batch, n_steps):
    g, r, w = u_g.shape
    per_g = lambda a: pl.BlockSpec((None,) + a.shape[1:], lambda i: (i, 0, 0))
    kern = functools.partial(_ssm_main_kernel, rows_per_batch=rows_per_batch, n_steps=n_steps)
    return pl.pallas_call(
        kern,
        grid=(g,),
        in_specs=[per_g(a) for a in (u_g, k_lag, bcat, ccat, apr, api, d_flat)],
        out_specs=per_g(u_g),
        out_shape=jax.ShapeDtypeStruct((g, r, w), F32),
        scratch_shapes=[pltpu.VMEM((w, w), BF16)],
        compiler_params=_cparams(("parallel",)),
        name="ssm_main",
    )(u_g, k_lag, bcat, ccat, apr, api, d_flat)


def _gelu(x):
    return 0.5 * x * (1.0 + lax.erf(x * (1.0 / math.sqrt(2.0))))


def _mix_rows(x_ref, att_ref, y_ref, wg_ref, bg_ref, ag_ref, sg_ref, wa_ref, ws_ref):
    y = _gelu(y_ref[...])
    z = _dot(y.astype(BF16), wg_ref[...]) + bg_ref[...]
    ssm = y * jax.nn.sigmoid(z)
    a = _rms_rows(att_ref[...], ag_ref[...])
    s = _rms_rows(ssm, sg_ref[...])
    return x_ref[...] + _dot(a.astype(BF16), wa_ref[...]) + _dot(s.astype(BF16), ws_ref[...])


def _mem_kv_kernel(mem_ref, g_ref, wk_ref, wv_ref, kg_ref, k_ref, v_ref):
    mn = _rms_rows(mem_ref[...], g_ref[...]).astype(BF16)
    k = _dot(mn, wk_ref[...])
    hd = k.shape[1] // MEM_HEADS
    for h in range(MEM_HEADS):
        k_ref[:, h * hd:(h + 1) * hd] = _rms_rows(k[:, h * hd:(h + 1) * hd], kg_ref[...]).astype(BF16)
    v_ref[...] = _dot(mn, wv_ref[...]).astype(BF16)


def _mem_kv(mem, g, wk_bf, wv_bf, kg):
    bsz, m, d = mem.shape
    fix = lambda b: (0, 0)
    per_b = pl.BlockSpec((None, m, d), lambda b: (b, 0, 0))
    return pl.pallas_call(
        _mem_kv_kernel,
        grid=(bsz,),
        in_specs=[per_b, pl.BlockSpec(g.shape, fix), pl.BlockSpec(wk_bf.shape, fix),
                  pl.BlockSpec(wv_bf.shape, fix), pl.BlockSpec(kg.shape, fix)],
        out_specs=[per_b, per_b],
        out_shape=[jax.ShapeDtypeStruct((bsz, m, d), BF16)] * 2,
        compiler_params=_cparams(("parallel",)),
        name="mem_kv",
    )(mem, g, wk_bf, wv_bf, kg)


def _mem_attn_kernel(x_ref, att_ref, y_ref, wg_ref, bg_ref, ag_ref, sg_ref, wa_ref, ws_ref,
                     g_ref, wq_ref, qg_ref, k_ref, v_ref, wo_ref, o_ref):
    h1 = _mix_rows(x_ref, att_ref, y_ref, wg_ref, bg_ref, ag_ref, sg_ref, wa_ref, ws_ref)
    hn = _rms_rows(h1, g_ref[...]).astype(BF16)
    q = _dot(hn, wq_ref[...])
    hd = q.shape[1] // MEM_HEADS
    outs = []
    for h in range(MEM_HEADS):
        sl = slice(h * hd, (h + 1) * hd)
        qh = (_rms_rows(q[:, sl], qg_ref[...]) * (hd ** -0.5)).astype(BF16)
        s = _dot_nt(qh, k_ref[:, sl])
        m = jnp.max(s, axis=-1, keepdims=True)
        p = jnp.exp(s - m)
        l = jnp.sum(p, axis=-1, keepdims=True)
        outs.append(_dot(p.astype(BF16), v_ref[:, sl]) / l)
    o = jnp.concatenate(outs, axis=-1).astype(BF16)
    o_ref[...] = (h1 + _dot(o, wo_ref[...])).T


def _mix_mem_attn(x3, att3, y3, mix_params, g, wq_bf, qg, kn, v, wo_bf):
    bsz, seq, d = x3.shape
    m = kn.shape[1]
    tm = MEM_ROWS
    steps = seq // tm
    fix = lambda b, i: (0, 0)
    tok = lambda a: pl.BlockSpec((None, tm, a.shape[-1]), lambda b, i: (b, i, 0))
    per_b = pl.BlockSpec((None, m, d), lambda b, i: (b, 0, 0))
    full = lambda a: pl.BlockSpec(a.shape, fix)
    return pl.pallas_call(
        _mem_attn_kernel,
        grid=(bsz, steps),
        in_specs=[tok(x3), tok(att3), tok(y3)] + [full(a) for a in mix_params]
        + [full(g), full(wq_bf), full(qg), per_b, per_b, full(wo_bf)],
        out_specs=pl.BlockSpec((d, tm), lambda b, i: (0, b * steps + i)),
        out_shape=jax.ShapeDtypeStruct((d, bsz * seq), F32),
        compiler_params=_cparams(("parallel", "parallel")),
        name="mix_mem_attn",
    )(x3, att3, y3, *mix_params, g, wq_bf, qg, kn, v, wo_bf)


_CAND_ROWS = 80


def _top16_rows(scores, exact):
    rows = scores.shape[0]
    ridx = lax.broadcasted_iota(jnp.int32, scores.shape, 0).astype(F32) if exact else None
    work = scores
    rank = jnp.full(scores.shape, float(PEER_TOPK), F32)
    vals = []
    for r in range(PEER_TOPK):
        m = jnp.max(work, axis=0, keepdims=True)
        hit = work == m
        if exact:
            first = jnp.min(jnp.where(hit, ridx, float(rows)), axis=0, keepdims=True)
            hit = ridx == first
        rank = jnp.where(hit, float(r), rank)
        work = jnp.where(hit, -jnp.inf, work)
        vals.append(m)
    ranked = jnp.sum(jnp.where(rank < float(PEER_TOPK), 1.0, 0.0), axis=0, keepdims=True)
    return jnp.concatenate(vals, axis=0), rank, ranked


def _peer_head(keys_ref, qt_ref, h, exact):
    nk = PEER_KEYS
    halves = []
    for s in range(2):
        r0 = (2 * h + s) * nk
        st = _dot(keys_ref[2 * h + s], qt_ref[r0:r0 + nk, :].astype(BF16))
        halves.append((st,) + _top16_rows(st, exact))
    (s0, sv0, rank0, ranked0), (s1, sv1, rank1, ranked1) = halves
    pieces = [sv0[0:1] + sv1]
    pieces += [sv0[a:a + 1] + sv1[0:8] for a in range(1, 8)]
    pieces += [sv0[8:16] + sv1[0:1]]
    cand = jnp.concatenate(pieces, axis=0)
    ridx = lax.broadcasted_iota(jnp.int32, cand.shape, 0).astype(F32) if exact else None
    work = cand
    for _ in range(PEER_TOPK):
        m = jnp.max(work, axis=0, keepdims=True)
        hit = work == m
        if exact:
            first = jnp.min(jnp.where(hit, ridx, float(_CAND_ROWS)), axis=0, keepdims=True)
            hit = ridx == first
        work = jnp.where(hit, -jnp.inf, work)
    sel = jnp.where(work == -jnp.inf, 1.0, 0.0)
    chosen = jnp.sum(sel, axis=0, keepdims=True)
    z = jnp.sum(sel * jnp.exp(cand - cand[0:1]), axis=0, keepdims=True)
    counts = [jnp.sum(sel[0:16], axis=0, keepdims=True)]
    counts += [jnp.sum(sel[16 + 8 * (a - 1):24 + 8 * (a - 1)], axis=0, keepdims=True)
               for a in range(1, 8)]
    counts += [sel[72 + a:73 + a] for a in range(8)]
    n_dense = jnp.zeros(rank0.shape, F32)
    for a in range(PEER_TOPK):
        n_dense = jnp.where(rank0 == float(a), counts[a], n_dense)
    a_dense = jnp.exp(s0 - sv0[0:1]) / z
    e_dense = jnp.exp(s1 - sv1[0:1])
    k = float(PEER_TOPK)
    clean = (ranked0 == k) & (ranked1 == k) & (chosen == k)
    return a_dense, n_dense, rank1, e_dense, clean


def _peer_select_kernel(ht_ref, g_ref, wq_ref, keys_ref, hn_ref, a0_ref, n0_ref, b1_ref, e1_ref,
                        qt_ref):
    ht = ht_ref[...]
    ms = jnp.mean(ht * ht, axis=0, keepdims=True)
    hn = (ht * lax.rsqrt(ms + EPS) * g_ref[...]).astype(BF16)
    hn_ref[...] = hn
    qt_ref[...] = _dot(wq_ref[...], hn)

    def write(h, exact):
        a_dense, n_dense, rank1, e_dense, clean = _peer_head(keys_ref, qt_ref, h, exact)
        a0_ref[h] = a_dense
        n0_ref[h] = n_dense
        b1_ref[h] = rank1.astype(BF16)
        e1_ref[h] = e_dense.astype(BF16)
        return clean

    def redo(h):
        write(h, exact=True)

    tied = [jnp.max(jnp.where(write(h, exact=False), 0.0, 1.0)) for h in range(PEER_HEADS)]
    for h in range(PEER_HEADS):
        pl.when(tied[h] > 0.0)(functools.partial(redo, h))


def _peer_select(h2t, g_col, wq_t_bf, keys_bf):
    d, n = h2t.shape
    tt = PEER_SEL_TOKENS
    col = lambda i: (0, i)
    dense = pl.BlockSpec((PEER_HEADS, PEER_KEYS, tt), lambda i: (0, 0, i))
    dense_shape = jax.ShapeDtypeStruct((PEER_HEADS, PEER_KEYS, n), BF16)
    per_key = dense
    per_key_shape = jax.ShapeDtypeStruct((PEER_HEADS, PEER_KEYS, n), F32)
    return pl.pallas_call(
        _peer_select_kernel,
        grid=(n // tt,),
        in_specs=[pl.BlockSpec((d, tt), col), pl.BlockSpec(g_col.shape, lambda i: (0, 0)),
                  pl.BlockSpec(wq_t_bf.shape, lambda i: (0, 0)),
                  pl.BlockSpec(keys_bf.shape, lambda i: (0, 0, 0))],
        out_specs=[pl.BlockSpec((d, tt), col), per_key, per_key, dense, dense],
        out_shape=[jax.ShapeDtypeStruct((d, n), BF16), per_key_shape, per_key_shape, dense_shape,
                   dense_shape],
        scratch_shapes=[pltpu.VMEM((wq_t_bf.shape[0], tt), F32)],
        compiler_params=_cparams(("parallel",)),
        name="peer_select",
    )(h2t, g_col, wq_t_bf, keys_bf)


def _peer_dense_kernel(ht_ref, hn_ref, u_ref, vt_ref, a0_ref, n0_ref, b1_ref, e1_ref, o_ref,
                       acc_ref, act0_ref, act1_ref, y0_ref, y1_ref):
    j = pl.program_id(1)

    @pl.when(j == 0)
    def _():
        acc_ref[...] = jnp.zeros_like(acc_ref)

    nk = PEER_KEYS
    kc = PEER_DENSE_CHUNK
    n_chunks = PEER_EXPERT_TILE // kc
    act_refs = (act0_ref, act1_ref)
    y_refs = (y0_ref, y1_ref)

    def activations(c):
        act_refs[c % 2][...] = _dot(u_ref[c * kc:(c + 1) * kc, :], hn_ref[...])

    def gate(c):
        act_ref, y_ref = act_refs[c % 2], y_refs[c % 2]
        gr = PEER_GATE_ROWS
        tokens = act_ref.shape[1]
        for ii in range(kc // nk):
            i = c * (kc // nk) + ii
            w = [None] * (nk // gr)
            for h in range(PEER_HEADS):
                n_i = jnp.broadcast_to(n0_ref[h, i:i + 1, :], (gr, tokens)).astype(BF16)
                a_i = jnp.broadcast_to(a0_ref[h, i:i + 1, :], (gr, tokens)).astype(BF16)
                for jb in range(nk // gr):
                    js = slice(jb * gr, (jb + 1) * gr)
                    term = a_i * jnp.where(b1_ref[h, js, :] < n_i, e1_ref[h, js, :], 0.0)
                    w[jb] = term if w[jb] is None else w[jb] + term
            for jb in range(nk // gr):
                rows = slice(ii * nk + jb * gr, ii * nk + (jb + 1) * gr)
                y_ref[rows, :] = w[jb] * _gelu(act_ref[rows, :]).astype(BF16)

    def project(c):
        acc_ref[...] += _dot(vt_ref[:, c * kc:(c + 1) * kc], y_refs[c % 2][...])

    activations(0)
    for c in range(n_chunks):
        if c + 1 < n_chunks:
            activations(c + 1)
        gate(c)
        if c >= 1:
            project(c - 1)
    project(n_chunks - 1)

    @pl.when(j == pl.num_programs(1) - 1)
    def _():
        o_ref[...] = (ht_ref[...] + acc_ref[...]).T


def _peer_dense(h2t, hn_t, u_bf, vt_bf, a0, n0, b1, e1):
    d, n = h2t.shape
    e = u_bf.shape[0]
    tt, te = PEER_TOKENS, PEER_EXPERT_TILE
    ipt = te // PEER_KEYS
    col = lambda i, j: (0, i)
    per_i = pl.BlockSpec((PEER_HEADS, ipt, tt), lambda i, j: (0, j, i))
    per_t = pl.BlockSpec((PEER_HEADS, PEER_KEYS, tt), lambda i, j: (0, 0, i))
    return pl.pallas_call(
        _peer_dense_kernel,
        grid=(n // tt, e // te),
        in_specs=[pl.BlockSpec((d, tt), col), pl.BlockSpec((d, tt), col),
                  pl.BlockSpec((te, d), lambda i, j: (j, 0)),
                  pl.BlockSpec((d, te), lambda i, j: (0, j)),
                  per_i, per_i, per_t, per_t],
        out_specs=pl.BlockSpec((tt, d), lambda i, j: (i, 0)),
        out_shape=jax.ShapeDtypeStruct((n, d), F32),
        scratch_shapes=[pltpu.VMEM((d, tt), F32)]
        + [pltpu.VMEM((PEER_DENSE_CHUNK, tt), F32)] * 2 + [pltpu.VMEM((PEER_DENSE_CHUNK, tt), BF16)] * 2,
        compiler_params=_cparams(("parallel", "arbitrary")),
        name="peer_dense",
    )(h2t, hn_t, u_bf, vt_bf, a0, n0, b1, e1)


def _layer(h, mem, p):
    bsz, seq, d = h.shape
    n = bsz * seq
    x2 = h.reshape(n, d)
    row = lambda a: a.reshape(1, -1)

    seg = jnp.asarray(np.kron(np.eye(ATT_HEADS), np.ones((ATT_HEAD_DIM, ATT_HEAD_DIM))), BF16)
    q, k, v, u = _inproj(
        x2, row(p["norm_mix_g"]), p["w_in"].astype(BF16), seg,
        row(jnp.tile(p["att_q_g"], ATT_HEADS)),
        row(jnp.tile(p["att_k_g"], ATT_HEADS)))
    shp = (bsz, seq, ATT_WIDTH)
    att = _attention(q.reshape(shp), k.reshape(shp), v.reshape(shp),
                     _band_bias_tile(p["rel_bias"]), bsz, seq).reshape(n, ATT_WIDTH)

    t, g, c = SSM_T, SSM_GROUPS, SSM_GROUP
    chunks = seq // t
    n_steps = max(1, int(math.ceil(math.log2(chunks))))
    k_lag, bcat, ccat, apr, api = _ssm_params(
        p["ssm_lam_re"], p["ssm_lam_im"], p["ssm_log_step"], p["ssm_b_re"], p["ssm_b_im"],
        p["ssm_c_re"], p["ssm_c_im"], n_steps)
    u_g = u.reshape(bsz, chunks, t, g, c).transpose(3, 0, 1, 2, 4).reshape(g, bsz * chunks, t * c)
    d_flat = jnp.tile(p["ssm_d"], (1, t))[:, None, :]
    y_g = _ssm_main(u_g, k_lag, bcat, ccat, apr, api, d_flat, chunks, n_steps)
    y = y_g.reshape(g, bsz, chunks, t, c).transpose(1, 2, 3, 0, 4).reshape(n, SSM_WIDTH)

    w_out = p["w_out"].astype(BF16)
    mix_params = (p["ssm_w_glu"].astype(BF16), row(p["ssm_b_glu"]), row(p["att_out_g"]),
                  row(p["ssm_out_g"]), w_out[:ATT_WIDTH], w_out[ATT_WIDTH:])

    kn, vm = _mem_kv(mem, row(p["norm_memkv_g"]), p["w_mem_k"].astype(BF16),
                     p["w_mem_v"].astype(BF16), row(p["mem_k_g"]))
    h2t = _mix_mem_attn(h, att.reshape(bsz, seq, ATT_WIDTH), y.reshape(bsz, seq, SSM_WIDTH),
                        mix_params, row(p["norm_mem_g"]), p["w_mem_q"].astype(BF16),
                        row(p["mem_q_g"]), kn, vm, p["w_mem_o"].astype(BF16))

    keys_bf = p["peer_keys"].reshape(PEER_HEADS * 2, PEER_KEYS, -1).astype(BF16)
    hn_t, a0, n0, b1, e1 = _peer_select(h2t, p["norm_peer_g"].reshape(d, 1),
                                        p["w_peer_q"].T.astype(BF16), keys_bf)
    h3 = _peer_dense(h2t, hn_t, p["peer_u"].astype(BF16), p["peer_v"].T.astype(BF16),
                     a0, n0, b1, e1)
    return h3.reshape(bsz, seq, d)


_PARAM_NAMES = (
    "norm_mix_g", "w_in", "att_q_g", "att_k_g", "rel_bias", "ssm_lam_re", "ssm_lam_im",
    "ssm_log_step", "ssm_b_re", "ssm_b_im", "ssm_c_re", "ssm_c_im", "ssm_d", "ssm_w_glu",
    "ssm_b_glu", "att_out_g", "ssm_out_g", "w_out", "norm_mem_g", "norm_memkv_g", "w_mem_q",
    "w_mem_k", "w_mem_v", "mem_q_g", "mem_k_g", "w_mem_o", "norm_peer_g", "w_peer_q",
    "peer_keys", "peer_u", "peer_v")


def kernel(x, mem, norm_mix_g, w_in, att_q_g, att_k_g, rel_bias, ssm_lam_re, ssm_lam_im, ssm_log_step, ssm_b_re, ssm_b_im, ssm_c_re, ssm_c_im, ssm_d, ssm_w_glu, ssm_b_glu, att_out_g, ssm_out_g, w_out, norm_mem_g, norm_memkv_g, w_mem_q, w_mem_k, w_mem_v, mem_q_g, mem_k_g, w_mem_o, norm_peer_g, w_peer_q, peer_keys, peer_u, peer_v):
    stacked = (norm_mix_g, w_in, att_q_g, att_k_g, rel_bias, ssm_lam_re, ssm_lam_im, ssm_log_step,
               ssm_b_re, ssm_b_im, ssm_c_re, ssm_c_im, ssm_d, ssm_w_glu, ssm_b_glu, att_out_g,
               ssm_out_g, w_out, norm_mem_g, norm_memkv_g, w_mem_q, w_mem_k, w_mem_v, mem_q_g,
               mem_k_g, w_mem_o, norm_peer_g, w_peer_q, peer_keys, peer_u, peer_v)
    h = x
    for l in range(norm_mix_g.shape[0]):
        h = _layer(h, mem, {name: a[l] for name, a in zip(_PARAM_NAMES, stacked)})
    return h
```
